```python
import math
import jax, jax.numpy as jnp
from jax import lax
import numpy as np

D_MODEL = 1024
BATCH = 4
SEQ = 4096
DEPTH = 4
DEC_BATCH = 128
DEC_SEQ = 8
PAST_LEN = 8192
PAGE_SIZE = 128

N_MIXERS = 2
N_SSD_LAYERS = (DEPTH + 1) // 2
N_MLA_LAYERS = DEPTH // 2
NORM_EPS = 1e-6
NEG_INF = -1e30

SSD_EXPAND = 2
SSD_D_INNER = SSD_EXPAND * D_MODEL
SSD_HEADDIM = 64
SSD_HEADS = SSD_D_INNER // SSD_HEADDIM
SSD_GROUPS = 8
SSD_STATE = 128
SSD_CONV_W = 4
SSD_CHUNK = 128
SSD_GN = SSD_GROUPS * SSD_STATE
SSD_CONV_DIM = SSD_D_INNER + 2 * SSD_GN
SSD_IN_DIM = SSD_D_INNER + SSD_CONV_DIM + SSD_HEADS

MLA_HEADS = 16
MLA_Q_LORA = 512
MLA_KV_LORA = 256
MLA_QK_NOPE = 64
MLA_QK_ROPE = 32
MLA_V_HEAD = 64
MLA_WIDTH = MLA_HEADS * MLA_V_HEAD
MLA_IN_DIM = MLA_Q_LORA + MLA_KV_LORA + MLA_QK_ROPE + MLA_WIDTH
MLA_SCALE = (MLA_QK_NOPE + MLA_QK_ROPE) ** -0.5
ROPE_BASE = 10000.0
Q_BLOCK = 128
POOL_SPARE_DIV = 4

kernel_name = 'hybrid_ssd_mla_decode_step'

F32 = jnp.float32


def rms_norm(x, w):
    xf = x.astype(F32)
    y = xf * lax.rsqrt(jnp.mean(xf * xf, axis=-1, keepdims=True) + NORM_EPS)
    return (y * w.astype(F32)).astype(x.dtype)


def causal_conv(xbc, buf, w, b):
    t = xbc.shape[1]
    full = jnp.concatenate([buf.astype(xbc.dtype), xbc], axis=1)
    out = full[:, 0:t] * w[0]
    for k in range(1, SSD_CONV_W):
        out = out + full[:, k:k + t] * w[k]
    return out + b, full[:, full.shape[1] - (SSD_CONV_W - 1):]


def ssd_scan(x, dt, a_head, bm, cm, init_state):
    b, L, h, p = x.shape
    g, n = bm.shape[2], bm.shape[3]
    r = h // g
    cl = math.gcd(L, SSD_CHUNK)
    nc = L // cl
    a = (dt * a_head).reshape(b, nc, cl, g, r).transpose(0, 3, 4, 1, 2)
    a_cs = jnp.cumsum(a, axis=-1)
    xdt = (x * dt[..., None]).reshape(b, nc, cl, g, r, p)
    bc = bm.reshape(b, nc, cl, g, n)
    cc = cm.reshape(b, nc, cl, g, n)
    seg = a_cs[..., :, None] - a_cs[..., None, :]
    causal = jnp.tril(jnp.ones((cl, cl), bool))
    decay = jnp.where(causal, jnp.exp(jnp.where(causal, seg, 0.0)), 0.0)
    cb = jnp.einsum('bclgn,bcsgn->bgcls', cc, bc)
    y_diag = jnp.einsum('bgrcls,bcsgrp->bclgrp', cb[:, :, None] * decay, xdt)
    decay_end = jnp.exp(a_cs[..., -1:] - a_cs).transpose(0, 3, 4, 1, 2)
    states = jnp.einsum('bcsgn,bcsgrp->bcgrpn', bc, xdt * decay_end[..., None])
    chunk_decay = jnp.exp(a_cs[..., -1])

    def step(s, inp):
        st, dec = inp
        return dec[..., None, None] * s + st, s

    final, prev = lax.scan(step, init_state.reshape(b, g, r, p, n),
                           (states.transpose(1, 0, 2, 3, 4, 5), chunk_decay.transpose(3, 0, 1, 2)))
    prev = prev.transpose(1, 0, 2, 3, 4, 5)
    decay_in = jnp.exp(a_cs).transpose(0, 3, 4, 1, 2)
    y_off = jnp.einsum('bclgn,bcgrpn->bclgrp', cc, prev) * decay_in[..., None]
    y = (y_diag + y_off).reshape(b, L, h, p)
    return y, final.reshape(b, h, p, n)


def ssd_mixer(h, conv_buf, ssm_state, w_in, conv_w, conv_b, dt_bias, a_log, d_skip, norm_w, w_out):
    b, t, _ = h.shape
    proj = jnp.einsum('btd,de->bte', h, w_in)
    z = proj[..., :SSD_D_INNER]
    xbc = proj[..., SSD_D_INNER:SSD_D_INNER + SSD_CONV_DIM]
    dt_raw = proj[..., SSD_D_INNER + SSD_CONV_DIM:]
    xbc, new_buf = causal_conv(xbc, conv_buf, conv_w, conv_b)
    xbc = jax.nn.silu(xbc.astype(F32))
    xs = xbc[..., :SSD_D_INNER].reshape(b, t, SSD_HEADS, SSD_HEADDIM)
    bm = xbc[..., SSD_D_INNER:SSD_D_INNER + SSD_GN].reshape(b, t, SSD_GROUPS, SSD_STATE)
    cm = xbc[..., SSD_D_INNER + SSD_GN:].reshape(b, t, SSD_GROUPS, SSD_STATE)
    dt = jax.nn.softplus(dt_raw.astype(F32) + dt_bias.astype(F32))
    a_head = -jnp.exp(a_log.astype(F32))
    y, new_state = ssd_scan(xs, dt, a_head, bm, cm, ssm_state.astype(F32))
    y = y + d_skip.astype(F32)[:, None] * xs
    y = y.reshape(b, t, SSD_D_INNER) * jax.nn.silu(z.astype(F32))
    yg = y.reshape(b, t, SSD_GROUPS, SSD_D_INNER // SSD_GROUPS)
    yg = yg * lax.rsqrt(jnp.mean(yg * yg, axis=-1, keepdims=True) + NORM_EPS)
    y = yg.reshape(b, t, SSD_D_INNER) * norm_w.astype(F32)
    out = jnp.einsum('bte,ed->btd', y.astype(h.dtype), w_out)
    return out, new_buf, new_state


def rope_tables(pos):
    inv = 1.0 / (ROPE_BASE ** (jnp.arange(0, MLA_QK_ROPE, 2, dtype=F32) / MLA_QK_ROPE))
    ang = pos[:, None] * inv[None, :]
    return jnp.cos(ang), jnp.sin(ang)


def apply_rope(x, cos, sin):
    xf = x.astype(F32)
    half = MLA_QK_ROPE // 2
    x1, x2 = xf[..., :half], xf[..., half:]
    return jnp.concatenate([x1 * cos - x2 * sin, x2 * cos + x1 * sin], axis=-1).astype(x.dtype)


def mla_project(h, pos, w_in, q_norm_w, kv_norm_w, w_uq):
    proj = jnp.einsum('btd,de->bte', h, w_in)
    o1 = MLA_Q_LORA
    o2 = o1 + MLA_KV_LORA
    o3 = o2 + MLA_QK_ROPE
    cq = rms_norm(proj[..., :o1], q_norm_w)
    ckv = rms_norm(proj[..., o1:o2], kv_norm_w)
    gate = proj[..., o3:]
    q = jnp.einsum('btq,qhe->bthe', cq, w_uq)
    cos, sin = rope_tables(pos)
    q_nope = q[..., :MLA_QK_NOPE]
    q_pe = apply_rope(q[..., MLA_QK_NOPE:], cos[:, None, :], sin[:, None, :])
    kpe = apply_rope(proj[..., o2:o3], cos, sin)
    return q_nope, q_pe, ckv, kpe, gate


def absorb(q_nope, w_uk):
    return jnp.einsum('bqhn,chn->bqhc', q_nope.astype(F32), w_uk.astype(F32))


def latent_partial(q_lat, q_pe, ckv, kpe, mask):
    s = (jnp.einsum('bqhc,bkc->bhqk', q_lat, ckv.astype(F32))
         + jnp.einsum('bqhr,bkr->bhqk', q_pe.astype(F32), kpe.astype(F32))) * MLA_SCALE
    if mask is not None:
        s = jnp.where(mask, s, NEG_INF)
    m = jnp.max(s, axis=-1)
    p = jnp.exp(s - m[..., None])
    return m, jnp.sum(p, axis=-1), jnp.einsum('bhqk,bkc->bhqc', p, ckv.astype(F32))


def merge_partial(pa, pb):
    ma, la, aa = pa
    mb, lb, ab = pb
    m = jnp.maximum(ma, mb)
    ca = jnp.exp(ma - m)
    cb = jnp.exp(mb - m)
    return m, la * ca + lb * cb, aa * ca[..., None] + ab * cb[..., None]


def latent_out(l, acc, w_uv):
    return jnp.einsum('bhqc,chv->bqhv', acc / l[..., None], w_uv.astype(F32))


def mla_prompt_attn(q_nope, q_pe, ckv, kpe, w_uk, w_uv):
    b, s = q_nope.shape[:2]
    nqb = s // Q_BLOCK
    k_pos = jnp.arange(s)

    def block(args):
        qn, qp, q_pos = args
        _, l, acc = latent_partial(absorb(qn, w_uk), qp, ckv, kpe, q_pos[:, None] >= k_pos[None, :])
        return latent_out(l, acc, w_uv)

    def to_blocks(a):
        return a.reshape((b, nqb, Q_BLOCK) + a.shape[2:]).swapaxes(0, 1)

    o = lax.map(block, (to_blocks(q_nope), to_blocks(q_pe), k_pos.reshape(nqb, Q_BLOCK)))
    return o.swapaxes(0, 1).reshape(b, s, MLA_HEADS, MLA_V_HEAD)


def mla_sample_attn(q_nope, q_pe, ckv, kpe, pool_ckv, pool_kpe, layer, page_table, w_uk, w_uv):
    t = q_nope.shape[1]
    q_lat = absorb(q_nope, w_uk)
    causal = jnp.tril(jnp.ones((t, t), bool))
    stats = latent_partial(q_lat, q_pe, ckv, kpe, causal)

    def page_step(carry, pt):
        page = latent_partial(q_lat, q_pe, pool_ckv[layer, pt], pool_kpe[layer, pt], None)
        return merge_partial(carry, page), None

    (_, l, acc), _ = lax.scan(page_step, stats, page_table.T)
    return latent_out(l, acc, w_uv)


def mla_output(o, gate, w_out):
    b, t = o.shape[:2]
    gated = o.reshape(b, t, MLA_WIDTH) * jax.nn.silu(gate.astype(F32))
    return jnp.einsum('bte,ed->btd', gated.astype(gate.dtype), w_out)


def setup_inputs(seed: int = 0) -> dict:
    key = jax.random.key(seed)
    ks = iter(jax.random.split(key, 32))

    def nrm(shape, scale):
        return jax.random.normal(next(ks), shape, F32) * scale

    n_pages = PAST_LEN // PAGE_SIZE
    n_used = DEC_BATCH * n_pages
    n_pool = n_used + n_used // POOL_SPARE_DIV
    x_prompt = nrm((BATCH, SEQ, D_MODEL), 1.0)
    x_sample = nrm((DEC_BATCH, DEC_SEQ, D_MODEL), 1.0)
    state_ssm = nrm((N_SSD_LAYERS, DEC_BATCH, SSD_HEADS, SSD_HEADDIM, SSD_STATE), 0.1)
    state_conv = nrm((N_SSD_LAYERS, DEC_BATCH, SSD_CONV_W - 1, SSD_CONV_DIM), 1.0)
    cache_ckv = nrm((N_MLA_LAYERS, n_pool, PAGE_SIZE, MLA_KV_LORA), 1.0)
    cache_kpe = nrm((N_MLA_LAYERS, n_pool, PAGE_SIZE, MLA_QK_ROPE), 1.0)
    page_table = jax.random.permutation(next(ks), n_pool)[:n_used].reshape(DEC_BATCH, n_pages).astype(jnp.int32)
    norm_w = 1.0 + nrm((DEPTH, D_MODEL), 0.02)
    final_norm_w = 1.0 + nrm((D_MODEL,), 0.02)
    ssd_w_in = nrm((N_SSD_LAYERS, D_MODEL, SSD_IN_DIM), D_MODEL ** -0.5)
    ssd_conv_w = nrm((N_SSD_LAYERS, SSD_CONV_W, SSD_CONV_DIM), SSD_CONV_W ** -0.5)
    ssd_conv_b = nrm((N_SSD_LAYERS, SSD_CONV_DIM), 0.01)
    dt0 = jnp.exp(jax.random.uniform(next(ks), (N_SSD_LAYERS, SSD_HEADS), F32,
                                     minval=math.log(1e-3), maxval=math.log(1e-1)))
    ssd_dt_bias = dt0 + jnp.log(-jnp.expm1(-dt0))
    ssd_a_log = jnp.log(jax.random.uniform(next(ks), (N_SSD_LAYERS, SSD_HEADS), F32, minval=1.0, maxval=16.0))
    ssd_d = 1.0 + nrm((N_SSD_LAYERS, SSD_HEADS), 0.1)
    ssd_norm_w = 1.0 + nrm((N_SSD_LAYERS, SSD_D_INNER), 0.02)
    ssd_w_out = nrm((N_SSD_LAYERS, SSD_D_INNER, D_MODEL), SSD_D_INNER ** -0.5)
    mla_w_in = nrm((N_MLA_LAYERS, D_MODEL, MLA_IN_DIM), D_MODEL ** -0.5)
    mla_q_norm_w = 1.0 + nrm((N_MLA_LAYERS, MLA_Q_LORA), 0.02)
    mla_kv_norm_w = 1.0 + nrm((N_MLA_LAYERS, MLA_KV_LORA), 0.02)
    mla_w_uq = nrm((N_MLA_LAYERS, MLA_Q_LORA, MLA_HEADS, MLA_QK_NOPE + MLA_QK_ROPE), MLA_Q_LORA ** -0.5)
    mla_w_uk = nrm((N_MLA_LAYERS, MLA_KV_LORA, MLA_HEADS, MLA_QK_NOPE), MLA_KV_LORA ** -0.5)
    mla_w_uv = nrm((N_MLA_LAYERS, MLA_KV_LORA, MLA_HEADS, MLA_V_HEAD), MLA_KV_LORA ** -0.5)
    mla_w_out = nrm((N_MLA_LAYERS, MLA_WIDTH, D_MODEL), MLA_WIDTH ** -0.5)
    return {'x_prompt': x_prompt, 'x_sample': x_sample, 'state_ssm': state_ssm, 'state_conv': state_conv,
            'cache_ckv': cache_ckv, 'cache_kpe': cache_kpe, 'page_table': page_table,
            'norm_w': norm_w, 'final_norm_w': final_norm_w,
            'ssd_w_in': ssd_w_in, 'ssd_conv_w': ssd_conv_w, 'ssd_conv_b': ssd_conv_b,
            'ssd_dt_bias': ssd_dt_bias, 'ssd_a_log': ssd_a_log, 'ssd_d': ssd_d,
            'ssd_norm_w': ssd_norm_w, 'ssd_w_out': ssd_w_out,
            'mla_w_in': mla_w_in, 'mla_q_norm_w': mla_q_norm_w, 'mla_kv_norm_w': mla_kv_norm_w,
            'mla_w_uq': mla_w_uq, 'mla_w_uk': mla_w_uk, 'mla_w_uv': mla_w_uv, 'mla_w_out': mla_w_out}


def reference(x_prompt, x_sample, state_ssm, state_conv, cache_ckv, cache_kpe, page_table,
              norm_w, final_norm_w,
              ssd_w_in, ssd_conv_w, ssd_conv_b, ssd_dt_bias, ssd_a_log, ssd_d, ssd_norm_w, ssd_w_out,
              mla_w_in, mla_q_norm_w, mla_kv_norm_w, mla_w_uq, mla_w_uk, mla_w_uv, mla_w_out):
    pos_p = jnp.arange(SEQ, dtype=F32)
    pos_s = PAST_LEN + jnp.arange(DEC_SEQ, dtype=F32)
    hp, hs = x_prompt, x_sample
    p_ssm, p_conv, p_ckv, p_kpe = [], [], [], []
    s_ssm, s_conv, s_ckv, s_kpe = [], [], [], []
    for i in range(DEPTH):
        j = i // N_MIXERS
        ln_p = rms_norm(hp, norm_w[i])
        ln_s = rms_norm(hs, norm_w[i])
        if i % N_MIXERS == 0:
            params = (ssd_w_in[j], ssd_conv_w[j], ssd_conv_b[j], ssd_dt_bias[j], ssd_a_log[j],
                      ssd_d[j], ssd_norm_w[j], ssd_w_out[j])
            zero_buf = jnp.zeros((BATCH, SSD_CONV_W - 1, SSD_CONV_DIM), ln_p.dtype)
            zero_ssm = jnp.zeros((BATCH, SSD_HEADS, SSD_HEADDIM, SSD_STATE), F32)
            out_p, buf_p, st_p = ssd_mixer(ln_p, zero_buf, zero_ssm, *params)
            out_s, buf_s, st_s = ssd_mixer(ln_s, state_conv[j], state_ssm[j], *params)
            p_ssm.append(st_p)
            p_conv.append(buf_p)
            s_ssm.append(st_s)
            s_conv.append(buf_s)
        else:
            qn_p, qp_p, ckv_p, kpe_p, gate_p = mla_project(ln_p, pos_p, mla_w_in[j], mla_q_norm_w[j],
                                                           mla_kv_norm_w[j], mla_w_uq[j])
            qn_s, qp_s, ckv_s, kpe_s, gate_s = mla_project(ln_s, pos_s, mla_w_in[j], mla_q_norm_w[j],
                                                           mla_kv_norm_w[j], mla_w_uq[j])
            o_p = mla_prompt_attn(qn_p, qp_p, ckv_p, kpe_p, mla_w_uk[j], mla_w_uv[j])
            o_s = mla_sample_attn(qn_s, qp_s, ckv_s, kpe_s, cache_ckv, cache_kpe, j, page_table,
                                  mla_w_uk[j], mla_w_uv[j])
            out_p = mla_output(o_p, gate_p, mla_w_out[j])
            out_s = mla_output(o_s, gate_s, mla_w_out[j])
            p_ckv.append(ckv_p)
            p_kpe.append(kpe_p)
            s_ckv.append(ckv_s)
            s_kpe.append(kpe_s)
        hp = hp + out_p.astype(hp.dtype)
        hs = hs + out_s.astype(hs.dtype)
    y_prompt = rms_norm(hp, final_norm_w)
    y_sample = rms_norm(hs, final_norm_w)
    return (y_prompt, y_sample,
            jnp.stack(p_ssm), jnp.stack(p_conv), jnp.stack(p_ckv), jnp.stack(p_kpe),
            jnp.stack(s_ssm), jnp.stack(s_conv), jnp.stack(s_ckv), jnp.stack(s_kpe))
```

```python
import functools
import math

import jax
import jax.numpy as jnp
from jax import lax
from jax.experimental import pallas as pl
from jax.experimental.pallas import tpu as pltpu

F32 = jnp.float32
BF16 = jnp.bfloat16

NORM_EPS = 1e-6
NEG_INF = -1e30
ROPE_BASE = 10000.0
LOG2E = 1.4426950408889634

LANES = 128
VMEM_LIMIT_BYTES = 56 * 1024 * 1024

SSD_CHUNK = 128
SSD_CONV_W = 4
PAGES_PER_STEP = 8


def _cparams(*sem):
    return pltpu.CompilerParams(dimension_semantics=sem, vmem_limit_bytes=VMEM_LIMIT_BYTES)


def _const_spec(shape):
    nd = len(shape)
    return pl.BlockSpec(shape, lambda *_: (0,) * nd, pipeline_mode=pl.Buffered(1))


def _sigmoid(x):
    return 1.0 / (1.0 + jnp.exp(-x))


def _silu(x):
    return x * _sigmoid(x)


def _dot(a, b):
    return jnp.dot(a, b, preferred_element_type=F32)


def _dot_nt(a, b):
    return lax.dot_general(a, b, (((1,), (1,)), ((), ())), preferred_element_type=F32)


def _rms(x, w):
    return x * lax.rsqrt(jnp.mean(x * x, axis=-1, keepdims=True) + NORM_EPS) * w


def _ssd_in_kernel(x_ref, nw_ref, w_ref, z_ref, xbc_ref, dt_ref, *, d_inner, conv_dim, n_chunk):
    xn = _rms(x_ref[...], nw_ref[...]).astype(BF16)
    col = 0
    for o_ref, width in ((z_ref, d_inner), (xbc_ref, conv_dim), (dt_ref, LANES)):
        for c0 in range(0, width, n_chunk):
            c1 = min(c0 + n_chunk, width)
            o_ref[:, c0:c1] = _dot(xn, w_ref[:, col + c0:col + c1]).astype(o_ref.dtype)
        col += width


def _ssd_in_proj(h, norm_w, w_big, d_inner, conv_dim, tm):
    t, d = h.shape
    n = w_big.shape[1]
    return pl.pallas_call(
        functools.partial(_ssd_in_kernel, d_inner=d_inner, conv_dim=conv_dim, n_chunk=1024),
        grid=(t // tm,),
        in_specs=[pl.BlockSpec((tm, d), lambda i: (i, 0)),
                  _const_spec((1, d)),
                  _const_spec((d, n))],
        out_specs=[pl.BlockSpec((tm, d_inner), lambda i: (i, 0)),
                   pl.BlockSpec((tm, conv_dim), lambda i: (i, 0)),
                   pl.BlockSpec((tm, LANES), lambda i: (i, 0))],
        out_shape=[jax.ShapeDtypeStruct((t, d_inner), F32),
                   jax.ShapeDtypeStruct((t, conv_dim), F32),
                   jax.ShapeDtypeStruct((t, LANES), F32)],
        compiler_params=_cparams("parallel"),
        name="ssd_in_proj",
    )(h, norm_w, w_big)


def _ssd_scan_kernel(*refs, rows, has_init, heads, headdim, groups, state):
    if has_init:
        (xbc_ref, z_ref, dt_ref, cinit_ref, sinit_ref, cw_ref, cb_ref, dtb_ref, alog_ref, dexp_ref,
         nw_ref, e_ref, y_ref, sout_ref, win_ref, st_ref) = refs
    else:
        (xbc_ref, z_ref, dt_ref, cw_ref, cb_ref, dtb_ref, alog_ref, dexp_ref,
         nw_ref, e_ref, y_ref, sout_ref, win_ref, st_ref) = refs
    L = SSD_CHUNK
    d_inner = heads * headdim
    hpg = heads // groups
    gw = hpg * headdim
    gn = groups * state
    c = pl.program_id(1)
    nc = pl.num_programs(1)

    @pl.when(c == 0)
    def _init():
        win_ref[0:8, :] = jnp.zeros((8, win_ref.shape[1]), F32)
        if has_init:
            win_ref[8 - (SSD_CONV_W - 1):8, :] = cinit_ref[0]
            for g in range(groups):
                sg = sinit_ref[0, g * hpg:(g + 1) * hpg].reshape(gw, state)
                st_ref[g] = sg.T
        else:
            st_ref[...] = jnp.zeros(st_ref.shape, F32)

    if rows < L:
        win_ref[8:8 + L, :] = jnp.zeros((L, win_ref.shape[1]), F32)
        win_ref[8:8 + rows, :] = xbc_ref[...]
    else:
        win_ref[8:8 + L, :] = xbc_ref[...]

    base = 8 - (SSD_CONV_W - 1)
    conv = cb_ref[...] + cw_ref[0:1, :] * win_ref[base:base + L, :]
    for k in range(1, SSD_CONV_W):
        conv = conv + cw_ref[k:k + 1, :] * win_ref[base + k:base + k + L, :]
    win_ref[0:8, :] = win_ref[L:L + 8, :]
    xc = _silu(conv)

    dt = dt_ref[...] + dtb_ref[...]
    dt = jnp.maximum(dt, 0.0) + jnp.log(1.0 + jnp.exp(-jnp.abs(dt)))
    if rows < L:
        dt = jnp.concatenate([dt, jnp.zeros((L - rows, LANES), F32)], axis=0)
    a = dt * (-jnp.exp(alog_ref[...]))

    ri = lax.broadcasted_iota(jnp.int32, (L, L), 0)
    ci = lax.broadcasted_iota(jnp.int32, (L, L), 1)
    causal = ri >= ci
    tril = causal.astype(F32)
    a_cs = jnp.dot(tril, a, precision=lax.Precision.HIGHEST, preferred_element_type=F32)
    a_cst = a_cs.T
    a_last = a_cs[L - 1:L, :]

    e = e_ref[...]

    def expand(v):
        hi = v.astype(BF16)
        lo = (v - hi.astype(F32)).astype(BF16)
        return _dot(hi, e) + _dot(lo, e)

    dt_x = expand(dt)
    din_x = expand(jnp.exp(a_cs))
    dend_x = expand(jnp.exp(a_last - a_cs))
    cdec_x = din_x[L - 1:L, :]

    rb = lax.broadcasted_iota(jnp.int32, (hpg * L, gw), 0) // L
    cbk = lax.broadcasted_iota(jnp.int32, (hpg * L, gw), 1) // headdim
    blockdiag = rb == cbk

    for g in range(groups):
        sl = slice(g * gw, (g + 1) * gw)
        xg = xc[:, sl]
        bg = xc[:, d_inner + g * state:d_inner + (g + 1) * state]
        cg = xc[:, d_inner + gn + g * state:d_inner + gn + (g + 1) * state]
        bg16 = bg.astype(BF16)
        cg16 = cg.astype(BF16)
        gmat = _dot_nt(cg16, bg16)
        ms = []
        for hh in range(hpg):
            hd = g * hpg + hh
            seg = a_cs[:, hd:hd + 1] - a_cst[hd:hd + 1, :]
            decay = jnp.where(causal, jnp.exp(jnp.where(causal, seg, 0.0)), 0.0)
            ms.append((gmat * decay).astype(BF16))
        mcat = jnp.concatenate(ms, axis=1)
        xdt = xg * dt_x[:, sl]
        xdt16 = xdt.astype(BF16)
        bd = jnp.where(blockdiag, jnp.concatenate([xdt16] * hpg, axis=0), jnp.zeros((), BF16))
        y_diag = _dot(mcat, bd)
        st = st_ref[g]
        y_off = _dot(cg16, st.astype(BF16)) * din_x[:, sl]
        y = y_diag + y_off + dexp_ref[:, sl] * xg
        xd = (xdt * dend_x[:, sl]).astype(BF16)
        st_ref[g] = st * cdec_x[:, sl] + _dot(bg.T.astype(BF16), xd)
        zg = z_ref[:, sl]
        yz = y[:rows] * _silu(zg)
        yn = yz * lax.rsqrt(jnp.mean(yz * yz, axis=-1, keepdims=True) + NORM_EPS) * nw_ref[:, sl]
        y_ref[:, sl] = yn.astype(y_ref.dtype)

    @pl.when(c == nc - 1)
    def _fin():
        for g in range(groups):
            sout_ref[0, g * hpg:(g + 1) * hpg] = st_ref[g].T.reshape(hpg, headdim, state)


def _ssd_scan(xbc, z, dt, row0, nseq, nchunks, rows, params, conv_init, ssm_init):
    cw, cb, dtb, alog, dexp, nw, e, dims = params
    heads, headdim, groups, state = dims
    d_inner = heads * headdim
    conv_dim = xbc.shape[1]
    has_init = conv_init is not None
    tok = lambda b, c: (row0 + b * nchunks + c, 0)
    out_tok = lambda b, c: (b * nchunks + c, 0)
    in_specs = [pl.BlockSpec((rows, conv_dim), tok),
                pl.BlockSpec((rows, d_inner), tok),
                pl.BlockSpec((rows, LANES), tok)]
    args = [xbc, z, dt]
    if has_init:
        in_specs += [pl.BlockSpec((1,) + conv_init.shape[1:], lambda b, c: (b, 0, 0)),
                     pl.BlockSpec((1,) + ssm_init.shape[1:], lambda b, c: (b, 0, 0, 0))]
        args += [conv_init, ssm_init]
    consts = [cw, cb, dtb, alog, dexp, nw, e]
    in_specs += [_const_spec(x.shape) for x in consts]
    args += consts
    return pl.pallas_call(
        functools.partial(_ssd_scan_kernel, rows=rows, has_init=has_init, heads=heads, headdim=headdim,
                          groups=groups, state=state),
        grid=(nseq, nchunks),
        in_specs=in_specs,
        out_specs=[pl.BlockSpec((rows, d_inner), out_tok),
                   pl.BlockSpec((1, heads, headdim, state), lambda b, c: (b, 0, 0, 0))],
        out_shape=[jax.ShapeDtypeStruct((nseq * nchunks * rows, d_inner), F32),
                   jax.ShapeDtypeStruct((nseq, heads, headdim, state), F32)],
        scratch_shapes=[pltpu.VMEM((SSD_CHUNK + 8 + 8, conv_dim), F32),
                        pltpu.VMEM((groups, state, (heads // groups) * headdim), F32)],
        compiler_params=_cparams("parallel", "arbitrary"),
        name="ssd_scan_init" if has_init else "ssd_scan",
    )(*args)


def _out_proj_kernel(*refs, final):
    if final:
        a_ref, w_ref, res_ref, fw_ref, o_ref = refs
    else:
        a_ref, w_ref, res_ref, o_ref = refs
    hnew = res_ref[...] + _dot(a_ref[...].astype(BF16), w_ref[...])
    if final:
        hnew = _rms(hnew, fw_ref[...])
    o_ref[...] = hnew


def _out_proj(a, w, res, final_w, tm):
    t, k = a.shape
    d = w.shape[1]
    final = final_w is not None
    in_specs = [pl.BlockSpec((tm, k), lambda i: (i, 0)), _const_spec((k, d)), pl.BlockSpec((tm, d), lambda i: (i, 0))]
    args = [a, w, res]
    if final:
        in_specs.append(_const_spec((1, d)))
        args.append(final_w)
    return pl.pallas_call(
        functools.partial(_out_proj_kernel, final=final),
        grid=(t // tm,),
        in_specs=in_specs,
        out_specs=pl.BlockSpec((tm, d), lambda i: (i, 0)),
        out_shape=jax.ShapeDtypeStruct((t, d), F32),
        compiler_params=_cparams("parallel"),
        name="out_proj_final" if final else "out_proj",
    )(*args)


def _mla_proj_kernel(*refs, prompt, heads, q_lora, kv_lora, rope, nope, scale):
    if prompt:
        (x_ref, nw_ref, win_ref, qnw_ref, kvnw_ref, wq_ref, wqr_ref, cosq_ref, sinq_ref, cosk_ref, sink_ref,
         wk_ref, wvt_ref, ckv_ref, kpe_ref, gate_ref, q_ref, k_ref, vt_ref) = refs
    else:
        (x_ref, nw_ref, win_ref, qnw_ref, kvnw_ref, wq_ref, wqr_ref, cosq_ref, sinq_ref, cosk_ref, sink_ref,
         wukt_ref, ckv_ref, kpe_ref, gate_ref, qlat_ref, qpe_ref) = refs
    xn = _rms(x_ref[...], nw_ref[...]).astype(BF16)
    o1 = q_lora
    o2 = o1 + kv_lora
    cq = _rms(_dot(xn, win_ref[:, 0:o1]), qnw_ref[...]).astype(BF16)
    ckv = _rms(_dot(xn, win_ref[:, o1:o2]), kvnw_ref[...])
    ckv_ref[...] = ckv
    kp = _dot(xn, win_ref[:, o2:o2 + 2 * LANES])
    kpe = kp[:, :LANES] * cosk_ref[...] + kp[:, LANES:] * sink_ref[...]
    kpe_ref[...] = kpe[:, :rope]
    gate_ref[...] = _dot(xn, win_ref[:, o2 + 2 * LANES:])
    hc = 8
    cosq = jnp.concatenate([cosq_ref[...] * scale] * hc, axis=1)
    sinq = jnp.concatenate([sinq_ref[...] * scale] * hc, axis=1)
    for h0 in range(0, heads, hc):
        cs = slice(h0 * LANES, (h0 + hc) * LANES)
        qc = _dot(cq, wq_ref[:, cs]) * cosq + _dot(cq, wqr_ref[:, cs]) * sinq
        if prompt:
            q_ref[:, cs] = qc.astype(BF16)
        else:
            for h in range(h0, h0 + hc):
                off = (h - h0) * LANES
                qlat_ref[h] = _dot(qc[:, off:off + nope].astype(BF16), wukt_ref[h])
                qpe_ref[h] = qc[:, off + nope:off + nope + rope]
    if prompt:
        ckv16 = ckv.astype(BF16)
        kin = jnp.concatenate([ckv16, kpe.astype(BF16)], axis=1)
        for c0 in range(0, heads * LANES, 1024):
            k_ref[:, c0:c0 + 1024] = _dot(kin, wk_ref[:, c0:c0 + 1024]).astype(BF16)
        vt_ref[0] = _dot_nt(wvt_ref[...], ckv16).astype(BF16)


def _mla_proj(h_all, row0_blocks, ntok, tm, seq_blocks, nw, w, tables, dims, prompt):
    heads, q_lora, kv_lora, rope, nope, vhead, scale = dims
    d = h_all.shape[1]
    cosq, sinq, cosk, sink = tables
    tok = lambda i: (row0_blocks + i, 0)
    out_tok = lambda i: (i, 0)
    pos = lambda i: (i % seq_blocks, 0)
    shared = [w["in"], w["qn"], w["kvn"], w["q"], w["qrot"]]
    in_specs = [pl.BlockSpec((tm, d), tok), _const_spec(nw.shape)] + [_const_spec(x.shape) for x in shared]
    in_specs += [pl.BlockSpec((tm, LANES), pos)] * 4
    args = [h_all, nw] + shared + [cosq, sinq, cosk, sink]
    gw = heads * vhead
    out_specs = [pl.BlockSpec((tm, kv_lora), out_tok), pl.BlockSpec((tm, rope), out_tok), pl.BlockSpec((tm, gw), out_tok)]
    out_shape = [jax.ShapeDtypeStruct((ntok, kv_lora), F32), jax.ShapeDtypeStruct((ntok, rope), F32),
                 jax.ShapeDtypeStruct((ntok, gw), F32)]
    if prompt:
        nb = ntok // (seq_blocks * tm)
        in_specs += [_const_spec(w["k"].shape), _const_spec(w["vt"].shape)]
        args += [w["k"], w["vt"]]
        out_specs += [pl.BlockSpec((tm, heads * LANES), out_tok), pl.BlockSpec((tm, heads * LANES), out_tok),
                      pl.BlockSpec((1, gw, tm), lambda i: (i // seq_blocks, 0, i % seq_blocks))]
        out_shape += [jax.ShapeDtypeStruct((ntok, heads * LANES), BF16), jax.ShapeDtypeStruct((ntok, heads * LANES), BF16),
                      jax.ShapeDtypeStruct((nb, gw, seq_blocks * tm), BF16)]
    else:
        in_specs += [_const_spec(w["ukt"].shape)]
        args += [w["ukt"]]
        out_specs += [pl.BlockSpec((heads, tm, kv_lora), lambda i: (0, i, 0)),
                      pl.BlockSpec((heads, tm, rope), lambda i: (0, i, 0))]
        out_shape += [jax.ShapeDtypeStruct((heads, ntok, kv_lora), F32), jax.ShapeDtypeStruct((heads, ntok, rope), F32)]
    return pl.pallas_call(
        functools.partial(_mla_proj_kernel, prompt=prompt, heads=heads, q_lora=q_lora, kv_lora=kv_lora, rope=rope,
                          nope=nope, scale=scale),
        grid=(ntok // tm,),
        in_specs=in_specs,
        out_specs=out_specs,
        out_shape=out_shape,
        compiler_params=_cparams("parallel"),
        name="mla_proj_prompt" if prompt else "mla_proj_sample",
    )(*args)


def _prompt_attn_kernel(qi_ref, ki_ref, q_ref, k_ref, vt_ref, gate_ref, o_ref, m_ref, l_ref, acc_ref, *,
                        heads, vhead, tq, tk):
    p = pl.program_id(1)
    qi = qi_ref[p]
    ki = ki_ref[p]

    @pl.when(ki == 0)
    def _init():
        m_ref[...] = jnp.full(m_ref.shape, NEG_INF, F32)
        l_ref[...] = jnp.zeros(l_ref.shape, F32)
        acc_ref[...] = jnp.zeros(acc_ref.shape, F32)

    kpos = ki * tk + lax.broadcasted_iota(jnp.int32, (tk, tq), 0)
    qpos = qi * tq + lax.broadcasted_iota(jnp.int32, (tk, tq), 1)
    visible = kpos <= qpos
    for h in range(heads):
        sl = slice(h * LANES, (h + 1) * LANES)
        st = _dot_nt(k_ref[:, sl], q_ref[:, sl])
        st = jnp.where(visible, st, NEG_INF)
        m_prev = m_ref[h]
        m_cur = jnp.maximum(m_prev, jnp.max(st, axis=0, keepdims=True))
        alpha = jnp.exp2(m_prev - m_cur)
        pt = jnp.exp2(st - m_cur)
        l_ref[h] = alpha * l_ref[h] + jnp.sum(pt, axis=0, keepdims=True)
        acc_ref[h] = acc_ref[h] * alpha + _dot(vt_ref[0, h * vhead:(h + 1) * vhead, :], pt.astype(BF16))
        m_ref[h] = m_cur

    @pl.when(ki == qi)
    def _fin():
        ot = (acc_ref[...] / l_ref[...]).reshape(heads * vhead, tq)
        o_ref[...] = ot.T * _silu(gate_ref[...])


def _prompt_attn(q, k, vt, gate, nb, seq, heads, vhead, tq, tk):
    nq = seq // tq
    pairs = [(a, b) for a in range(nq) for b in range(a * tq // tk + 1) if b * tk <= a * tq + tq - 1]
    qi_tab = jnp.asarray([a for a, _ in pairs], jnp.int32)
    ki_tab = jnp.asarray([b for _, b in pairs], jnp.int32)
    nk = seq // tk
    grid_spec = pltpu.PrefetchScalarGridSpec(
        num_scalar_prefetch=2,
        grid=(nb, len(pairs)),
        in_specs=[pl.BlockSpec((tq, heads * LANES), lambda b, p, qt, kt: (b * nq + qt[p], 0)),
                  pl.BlockSpec((tk, heads * LANES), lambda b, p, qt, kt: (b * nk + kt[p], 0)),
                  pl.BlockSpec((1, heads * vhead, tk), lambda b, p, qt, kt: (b, 0, kt[p])),
                  pl.BlockSpec((tq, heads * vhead), lambda b, p, qt, kt: (b * nq + qt[p], 0))],
        out_specs=pl.BlockSpec((tq, heads * vhead), lambda b, p, qt, kt: (b * nq + qt[p], 0)),
        scratch_shapes=[pltpu.VMEM((heads, 1, tq), F32), pltpu.VMEM((heads, 1, tq), F32),
                        pltpu.VMEM((heads, vhead, tq), F32)])
    return pl.pallas_call(
        functools.partial(_prompt_attn_kernel, heads=heads, vhead=vhead, tq=tq, tk=tk),
        grid_spec=grid_spec,
        out_shape=jax.ShapeDtypeStruct((nb * seq, heads * vhead), F32),
        compiler_params=_cparams("parallel", "arbitrary"),
        name="mla_prompt_attn",
    )(qi_tab, ki_tab, q, k, vt, gate)


def _sample_attn_kernel(pt_ref, qlat_ref, qpe_ref, ckvn_ref, kpen_ref, *refs, npages, heads, tnew):
    ckv_refs = refs[:npages]
    kpe_refs = refs[npages:2 * npages]
    o_ref, m_ref, l_ref, acc_ref = refs[2 * npages:]
    s = pl.program_id(1)
    rows = heads * tnew
    q = qlat_ref[...].reshape(rows, qlat_ref.shape[2]).astype(BF16)
    qp = qpe_ref[...].reshape(rows, qpe_ref.shape[2]).astype(BF16)

    @pl.when(s == 0)
    def _new_tokens():
        cn = ckvn_ref[...].astype(BF16)
        sc = _dot_nt(q, cn) + _dot_nt(qp, kpen_ref[...].astype(BF16))
        tq_ = lax.broadcasted_iota(jnp.int32, (rows, tnew), 0) % tnew
        tk_ = lax.broadcasted_iota(jnp.int32, (rows, tnew), 1)
        sc = jnp.where(tk_ <= tq_, sc, NEG_INF)
        m = jnp.max(sc, axis=-1, keepdims=True)
        pr = jnp.exp2(sc - m)
        m_ref[...] = m
        l_ref[...] = jnp.sum(pr, axis=-1, keepdims=True)
        acc_ref[...] = _dot(pr, ckvn_ref[...])

    for i in range(npages):
        ck = ckv_refs[i][...].astype(BF16)
        sc = _dot_nt(q, ck) + _dot_nt(qp, kpe_refs[i][...].astype(BF16))
        m_prev = m_ref[...]
        m_cur = jnp.maximum(m_prev, jnp.max(sc, axis=-1, keepdims=True))
        alpha = jnp.exp2(m_prev - m_cur)
        pr = jnp.exp2(sc - m_cur)
        l_ref[...] = alpha * l_ref[...] + jnp.sum(pr, axis=-1, keepdims=True)
        acc_ref[...] = acc_ref[...] * alpha + _dot(pr.astype(BF16), ck)
        m_ref[...] = m_cur

    @pl.when(s == pl.num_programs(1) - 1)
    def _fin():
        o_ref[0] = acc_ref[...] / l_ref[...]


def _sample_attn(page_table, qlat, qpe, ckv_new, kpe_new, cache_ckv, cache_kpe, layer, npages):
    heads, ntok, kv_lora = qlat.shape
    rope = qpe.shape[2]
    nb, n_pages_total = page_table.shape
    tnew = ntok // nb
    page = cache_ckv.shape[2]
    steps = n_pages_total // npages
    rows = heads * tnew

    def page_map(i):
        return lambda b, s, pt: (layer, pt[b, s * npages + i], 0, 0)

    in_specs = [pl.BlockSpec((heads, tnew, kv_lora), lambda b, s, pt: (0, b, 0)),
                pl.BlockSpec((heads, tnew, rope), lambda b, s, pt: (0, b, 0)),
                pl.BlockSpec((tnew, kv_lora), lambda b, s, pt: (b, 0)),
                pl.BlockSpec((tnew, rope), lambda b, s, pt: (b, 0))]
    in_specs += [pl.BlockSpec((None, None, page, kv_lora), page_map(i)) for i in range(npages)]
    in_specs += [pl.BlockSpec((None, None, page, rope), page_map(i)) for i in range(npages)]
    grid_spec = pltpu.PrefetchScalarGridSpec(
        num_scalar_prefetch=1,
        grid=(nb, steps),
        in_specs=in_specs,
        out_specs=pl.BlockSpec((1, rows, kv_lora), lambda b, s, pt: (b, 0, 0)),
        scratch_shapes=[pltpu.VMEM((rows, 1), F32), pltpu.VMEM((rows, 1), F32), pltpu.VMEM((rows, kv_lora), F32)])
    return pl.pallas_call(
        functools.partial(_sample_attn_kernel, npages=npages, heads=heads, tnew=tnew),
        grid_spec=grid_spec,
        out_shape=jax.ShapeDtypeStruct((nb, rows, kv_lora), F32),
        compiler_params=_cparams("parallel", "arbitrary"),
        name="mla_sample_attn",
    )(page_table, qlat, qpe, ckv_new, kpe_new, *([cache_ckv] * npages), *([cache_kpe] * npages))


def _sample_post_kernel(ol_ref, wuv_ref, gate_ref, o_ref, *, heads, tnew, vhead):
    bt = ol_ref.shape[0]
    for h in range(heads):
        x = ol_ref[:, h * tnew:(h + 1) * tnew, :].reshape(bt * tnew, ol_ref.shape[2]).astype(BF16)
        g = gate_ref[:, h * vhead:(h + 1) * vhead]
        o_ref[:, h * vhead:(h + 1) * vhead] = _dot(x, wuv_ref[h]) * _silu(g)


def _sample_post(olat, wuv, gate, tnew, bt):
    nb, rows, kv_lora = olat.shape
    heads, _, vhead = wuv.shape
    ntok = nb * tnew
    return pl.pallas_call(
        functools.partial(_sample_post_kernel, heads=heads, tnew=tnew, vhead=vhead),
        grid=(nb // bt,),
        in_specs=[pl.BlockSpec((bt, rows, kv_lora), lambda i: (i, 0, 0)),
                  _const_spec(wuv.shape),
                  pl.BlockSpec((bt * tnew, heads * vhead), lambda i: (i, 0))],
        out_specs=pl.BlockSpec((bt * tnew, heads * vhead), lambda i: (i, 0)),
        out_shape=jax.ShapeDtypeStruct((ntok, heads * vhead), F32),
        compiler_params=_cparams("parallel"),
        name="mla_sample_post",
    )(olat, wuv, gate)


def _rope_tables(pos, rope, nope):
    half = rope // 2
    inv = 1.0 / (ROPE_BASE ** (jnp.arange(0, rope, 2, dtype=F32) / rope))
    ang = pos[:, None] * inv[None, :]
    cos, sin = jnp.cos(ang), jnp.sin(ang)
    n = pos.shape[0]
    cc = jnp.concatenate([cos, cos], axis=1)
    ss = jnp.concatenate([sin, sin], axis=1)
    cosk = jnp.pad(cc, ((0, 0), (0, LANES - rope)))
    sink = jnp.pad(ss, ((0, 0), (0, LANES - rope)))
    cosq = jnp.concatenate([jnp.ones((n, nope), F32), cc, jnp.zeros((n, LANES - nope - rope), F32)], axis=1)
    sinq = jnp.concatenate([jnp.zeros((n, nope), F32), ss, jnp.zeros((n, LANES - nope - rope), F32)], axis=1)
    del half
    return cosq, sinq, cosk, sink


def _rot_cols(w, half):
    return jnp.concatenate([-w[..., half:], w[..., :half]], axis=-1)


def _mla_weights(w_in, qn, kvn, w_uq, w_uk, w_uv, q_lora, kv_lora, rope, nope):
    d = w_in.shape[0]
    heads = w_uq.shape[1]
    vhead = w_uv.shape[2]
    o1, o2, o3 = q_lora, q_lora + kv_lora, q_lora + kv_lora + rope
    pad = LANES - rope
    wkpe = w_in[:, o2:o3]
    w_in_big = jnp.concatenate([w_in[:, :o2], jnp.pad(wkpe, ((0, 0), (0, pad))),
                                jnp.pad(_rot_cols(wkpe, rope // 2), ((0, 0), (0, pad))), w_in[:, o3:]], axis=1)
    q_nope, q_pe = w_uq[:, :, :nope], w_uq[:, :, nope:]
    zpad = jnp.zeros((q_lora, heads, LANES - nope - rope), F32)
    wq = jnp.concatenate([q_nope, q_pe, zpad], axis=2).reshape(q_lora, heads * LANES)
    wqr = jnp.concatenate([jnp.zeros_like(q_nope), _rot_cols(q_pe, rope // 2), zpad], axis=2).reshape(q_lora, heads * LANES)
    wk_lat = jnp.concatenate([w_uk, jnp.zeros((kv_lora, heads, LANES - nope), F32)], axis=2).reshape(kv_lora, heads * LANES)
    place = jnp.zeros((LANES, heads, LANES), F32).at[jnp.arange(rope), :, nope + jnp.arange(rope)].set(1.0)
    wk = jnp.concatenate([wk_lat, place.reshape(LANES, heads * LANES)], axis=0)
    wvt = w_uv.transpose(1, 2, 0).reshape(heads * vhead, kv_lora)
    return {"in": w_in_big.astype(BF16), "qn": qn.reshape(1, -1), "kvn": kvn.reshape(1, -1),
            "q": wq.astype(BF16), "qrot": wqr.astype(BF16), "k": wk.astype(BF16), "vt": wvt.astype(BF16),
            "ukt": w_uk.transpose(1, 2, 0).astype(BF16), "uv": w_uv.transpose(1, 0, 2).astype(BF16)}


def kernel(x_prompt, x_sample, state_ssm, state_conv, cache_ckv, cache_kpe, page_table, norm_w, final_norm_w,
           ssd_w_in, ssd_conv_w, ssd_conv_b, ssd_dt_bias, ssd_a_log, ssd_d, ssd_norm_w, ssd_w_out,
           mla_w_in, mla_q_norm_w, mla_kv_norm_w, mla_w_uq, mla_w_uk, mla_w_uv, mla_w_out):
    nb, seq, d = x_prompt.shape
    db, dseq, _ = x_sample.shape
    depth = norm_w.shape[0]
    tp, ts = nb * seq, db * dseq
    _, _, heads_s, headdim, state = state_ssm.shape
    d_inner = heads_s * headdim
    conv_dim = ssd_conv_w.shape[2]
    gn = (conv_dim - d_inner) // 2
    groups = gn // state
    q_lora, heads_m, qk = mla_w_uq.shape[1:]
    kv_lora = mla_w_uk.shape[1]
    nope = mla_w_uk.shape[3]
    rope = qk - nope
    vhead = mla_w_uv.shape[3]
    past = page_table.shape[1] * cache_ckv.shape[2]
    scale = (nope + rope) ** -0.5 * LOG2E

    assert seq % SSD_CHUNK == 0 and dseq <= SSD_CHUNK and tp % dseq == 0
    tm = 512
    assert tp % tm == 0 and ts % tm == 0

    h_all = jnp.concatenate([x_prompt.reshape(tp, d), x_sample.reshape(ts, d)], axis=0)

    e = (jnp.arange(d_inner)[None, :] // headdim == jnp.arange(LANES)[:, None]).astype(BF16)
    tab_p = _rope_tables(jnp.arange(seq, dtype=F32), rope, nope)
    tab_s = _rope_tables(jnp.tile(past + jnp.arange(dseq, dtype=F32), ts // dseq), rope, nope)
    mla_dims = (heads_m, q_lora, kv_lora, rope, nope, vhead, scale)

    p_ssm, p_conv, p_ckv, p_kpe = [], [], [], []
    s_ssm, s_conv, s_ckv, s_kpe = [], [], [], []
    for i in range(depth):
        j = i // 2
        nw = norm_w[i].reshape(1, d)
        last = i == depth - 1
        if i % 2 == 0:
            w_big = jnp.pad(ssd_w_in[j], ((0, 0), (0, LANES - heads_s))).astype(BF16)
            z, xbc, dt = _ssd_in_proj(h_all, nw, w_big, d_inner, conv_dim, tm)
            params = (ssd_conv_w[j], ssd_conv_b[j].reshape(1, -1),
                      jnp.pad(ssd_dt_bias[j], (0, LANES - heads_s)).reshape(1, LANES),
                      jnp.pad(ssd_a_log[j], (0, LANES - heads_s)).reshape(1, LANES),
                      jnp.repeat(ssd_d[j], headdim).reshape(1, d_inner),
                      ssd_norm_w[j].reshape(1, d_inner), e, (heads_s, headdim, groups, state))
            y_p, st_p = _ssd_scan(xbc, z, dt, 0, nb, seq // SSD_CHUNK, SSD_CHUNK, params, None, None)
            y_s, st_s = _ssd_scan(xbc, z, dt, tp // dseq, db, 1, dseq, params, state_conv[j], state_ssm[j])
            y = jnp.concatenate([y_p, y_s], axis=0)
            h_all = _out_proj(y, ssd_w_out[j].astype(BF16), h_all, final_norm_w.reshape(1, d) if last else None, tm)
            xbc_p = xbc[:tp].reshape(nb, seq, conv_dim)
            xbc_s = xbc[tp:].reshape(db, dseq, conv_dim)
            p_ssm.append(st_p)
            p_conv.append(xbc_p[:, seq - (SSD_CONV_W - 1):])
            s_ssm.append(st_s)
            s_conv.append(xbc_s[:, dseq - (SSD_CONV_W - 1):])
        else:
            w = _mla_weights(mla_w_in[j], mla_q_norm_w[j], mla_kv_norm_w[j], mla_w_uq[j], mla_w_uk[j], mla_w_uv[j],
                             q_lora, kv_lora, rope, nope)
            tq = 256
            ckv_p, kpe_p, gate_p, q_p, k_p, vt_p = _mla_proj(h_all, 0, tp, tq, seq // tq, nw, w, tab_p, mla_dims, True)
            tms = 256
            ckv_s, kpe_s, gate_s, qlat_s, qpe_s = _mla_proj(h_all, tp // tms, ts, tms, ts // tms, nw, w, tab_s,
                                                            mla_dims, False)
            og_p = _prompt_attn(q_p, k_p, vt_p, gate_p, nb, seq, heads_m, vhead, tq, tq)
            npg = math.gcd(PAGES_PER_STEP, page_table.shape[1])
            olat = _sample_attn(page_table, qlat_s, qpe_s, ckv_s, kpe_s, cache_ckv, cache_kpe, j, npg)
            og_s = _sample_post(olat, w["uv"], gate_s, dseq, math.gcd(16, db))
            og = jnp.concatenate([og_p, og_s], axis=0)
            h_all = _out_proj(og, mla_w_out[j].astype(BF16), h_all, final_norm_w.reshape(1, d) if last else None, tm)
            p_ckv.append(ckv_p.reshape(nb, seq, kv_lora))
            p_kpe.append(kpe_p.reshape(nb, seq, rope))
            s_ckv.append(ckv_s.reshape(db, dseq, kv_lora))
            s_kpe.append(kpe_s.reshape(db, dseq, rope))
    y_prompt = h_all[:tp].reshape(nb, seq, d)
    y_sample = h_all[tp:].reshape(db, dseq, d)
    return (y_prompt, y_sample,
            jnp.stack(p_ssm), jnp.stack(p_conv), jnp.stack(p_ckv), jnp.stack(p_kpe),
            jnp.stack(s_ssm), jnp.stack(s_conv), jnp.stack(s_ckv), jnp.stack(s_kpe))
```

```python
import functools
import math

import jax
import jax.numpy as jnp
from jax import lax
from jax.experimental import pallas as pl
from jax.experimental.pallas import tpu as pltpu

F32 = jnp.float32
BF16 = jnp.bfloat16

NORM_EPS = 1e-6
NEG_INF = -1e30
ROPE_BASE = 10000.0
LOG2E = 1.4426950408889634

LANES = 128
VMEM_LIMIT_BYTES = 56 * 1024 * 1024

SSD_CHUNK = 128
SSD_CONV_W = 4
PAGES_PER_STEP = 16
ATTN_TILE = 512


def _cparams(*sem):
    return pltpu.CompilerParams(dimension_semantics=sem, vmem_limit_bytes=VMEM_LIMIT_BYTES)


def _const_spec(shape):
    nd = len(shape)
    return pl.BlockSpec(shape, lambda *_: (0,) * nd, pipeline_mode=pl.Buffered(1))


def _sigmoid(x):
    return 1.0 / (1.0 + jnp.exp(-x))


def _silu(x):
    return x * _sigmoid(x)


def _dot(a, b):
    return jnp.dot(a, b, preferred_element_type=F32)


def _dot_nt(a, b):
    return lax.dot_general(a, b, (((1,), (1,)), ((), ())), preferred_element_type=F32)


def _rms(x, w):
    return x * lax.rsqrt(jnp.mean(x * x, axis=-1, keepdims=True) + NORM_EPS) * w


def _ssd_in_kernel(x_ref, nw_ref, w_ref, z_ref, xbc_ref, dt_ref, *, d_inner, conv_dim, n_chunk):
    xn = _rms(x_ref[...], nw_ref[...]).astype(BF16)
    col = 0
    for o_ref, width in ((z_ref, d_inner), (xbc_ref, conv_dim), (dt_ref, LANES)):
        for c0 in range(0, width, n_chunk):
            c1 = min(c0 + n_chunk, width)
            o_ref[:, c0:c1] = _dot(xn, w_ref[:, col + c0:col + c1]).astype(o_ref.dtype)
        col += width


def _ssd_in_proj(h, norm_w, w_big, d_inner, conv_dim, tm):
    t, d = h.shape
    n = w_big.shape[1]
    return pl.pallas_call(
        functools.partial(_ssd_in_kernel, d_inner=d_inner, conv_dim=conv_dim, n_chunk=1024),
        grid=(t // tm,),
        in_specs=[pl.BlockSpec((tm, d), lambda i: (i, 0)),
                  _const_spec((1, d)),
                  _const_spec((d, n))],
        out_specs=[pl.BlockSpec((tm, d_inner), lambda i: (i, 0)),
                   pl.BlockSpec((tm, conv_dim), lambda i: (i, 0)),
                   pl.BlockSpec((tm, LANES), lambda i: (i, 0))],
        out_shape=[jax.ShapeDtypeStruct((t, d_inner), F32),
                   jax.ShapeDtypeStruct((t, conv_dim), F32),
                   jax.ShapeDtypeStruct((t, LANES), F32)],
        compiler_params=_cparams("parallel"),
        name="ssd_in_proj",
    )(h, norm_w, w_big)


def _ssd_scan_kernel(*refs, rows, has_init, has_prev, heads, headdim, groups, state):
    if has_prev:
        refs = refs[:-5] + refs[-4:]
    if has_init:
        (xbc_ref, z_ref, dt_ref, cinit_ref, sinit_ref, cw_ref, cb_ref, dtb_ref, alog_ref, dexp_ref,
         nw_ref, e_ref, y_ref, sout_ref, win_ref, st_ref) = refs
    else:
        (xbc_ref, z_ref, dt_ref, cw_ref, cb_ref, dtb_ref, alog_ref, dexp_ref,
         nw_ref, e_ref, y_ref, sout_ref, win_ref, st_ref) = refs
    L = SSD_CHUNK
    d_inner = heads * headdim
    hpg = heads // groups
    gw = hpg * headdim
    gn = groups * state
    c = pl.program_id(1)
    nc = pl.num_programs(1)

    @pl.when(c == 0)
    def _init():
        win_ref[0:8, :] = jnp.zeros((8, win_ref.shape[1]), F32)
        if has_init:
            win_ref[8 - (SSD_CONV_W - 1):8, :] = cinit_ref[0]
            for g in range(groups):
                sg = sinit_ref[0, g * hpg:(g + 1) * hpg].reshape(gw, state)
                st_ref[g] = sg.T
        else:
            st_ref[...] = jnp.zeros(st_ref.shape, F32)

    if rows < L:
        win_ref[8:8 + L, :] = jnp.zeros((L, win_ref.shape[1]), F32)
        win_ref[8:8 + rows, :] = xbc_ref[...]
    else:
        win_ref[8:8 + L, :] = xbc_ref[...]

    base = 8 - (SSD_CONV_W - 1)
    conv = cb_ref[...] + cw_ref[0:1, :] * win_ref[base:base + L, :]
    for k in range(1, SSD_CONV_W):
        conv = conv + cw_ref[k:k + 1, :] * win_ref[base + k:base + k + L, :]
    win_ref[0:8, :] = win_ref[L:L + 8, :]
    xc = _silu(conv)

    dt = dt_ref[...] + dtb_ref[...]
    dt = jnp.maximum(dt, 0.0) + jnp.log(1.0 + jnp.exp(-jnp.abs(dt)))
    if rows < L:
        dt = jnp.concatenate([dt, jnp.zeros((L - rows, LANES), F32)], axis=0)
    a = dt * (-jnp.exp(alog_ref[...]))

    ri = lax.broadcasted_iota(jnp.int32, (L, L), 0)
    ci = lax.broadcasted_iota(jnp.int32, (L, L), 1)
    causal = ri >= ci
    tril = causal.astype(F32)
    a_cs = jnp.dot(tril, a, precision=lax.Precision.HIGHEST, preferred_element_type=F32)
    a_cst = a_cs.T
    a_last = a_cs[L - 1:L, :]

    e = e_ref[...]

    def expand(v):
        hi = v.astype(BF16)
        lo = (v - hi.astype(F32)).astype(BF16)
        return _dot(hi, e) + _dot(lo, e)

    dt_x = expand(dt)
    din_x = expand(jnp.exp(a_cs))
    dend_x = expand(jnp.exp(a_last - a_cs))
    cdec_x = din_x[L - 1:L, :]

    rb = lax.broadcasted_iota(jnp.int32, (hpg * L, gw), 0) // L
    cbk = lax.broadcasted_iota(jnp.int32, (hpg * L, gw), 1) // headdim
    blockdiag = rb == cbk

    for g in range(groups):
        sl = slice(g * gw, (g + 1) * gw)
        xg = xc[:, sl]
        bg = xc[:, d_inner + g * state:d_inner + (g + 1) * state]
        cg = xc[:, d_inner + gn + g * state:d_inner + gn + (g + 1) * state]
        bg16 = bg.astype(BF16)
        cg16 = cg.astype(BF16)
        gmat = _dot_nt(cg16, bg16)
        ms = []
        for hh in range(hpg):
            hd = g * hpg + hh
            seg = a_cs[:, hd:hd + 1] - a_cst[hd:hd + 1, :]
            decay = jnp.where(causal, jnp.exp(jnp.where(causal, seg, 0.0)), 0.0)
            ms.append((gmat * decay).astype(BF16))
        mcat = jnp.concatenate(ms, axis=1)
        xdt = xg * dt_x[:, sl]
        xdt16 = xdt.astype(BF16)
        bd = jnp.where(blockdiag, jnp.concatenate([xdt16] * hpg, axis=0), jnp.zeros((), BF16))
        y_diag = _dot(mcat, bd)
        st = st_ref[g]
        y_off = _dot(cg16, st.astype(BF16)) * din_x[:, sl]
        y = y_diag + y_off + dexp_ref[:, sl] * xg
        xd = (xdt * dend_x[:, sl]).astype(BF16)
        st_ref[g] = st * cdec_x[:, sl] + _dot(bg.T.astype(BF16), xd)
        zg = z_ref[:, sl]
        yz = y[:rows] * _silu(zg)
        yn = yz * lax.rsqrt(jnp.mean(yz * yz, axis=-1, keepdims=True) + NORM_EPS) * nw_ref[:, sl]
        y_ref[:, sl] = yn.astype(y_ref.dtype)

    @pl.when(c == nc - 1)
    def _fin():
        for g in range(groups):
            sout_ref[0, g * hpg:(g + 1) * hpg] = st_ref[g].T.reshape(hpg, headdim, state)


def _ssd_scan(xbc, z, dt, row0, nseq, nchunks, rows, params, layer, n_layers, conv_init, ssm_init, state_out):
    cw, cb, dtb, alog, dexp, nw, e, dims = params
    heads, headdim, groups, state = dims
    d_inner = heads * headdim
    conv_dim = xbc.shape[1]
    has_init = conv_init is not None
    tok = lambda b, c: (row0 + b * nchunks + c, 0)
    out_tok = lambda b, c: (b * nchunks + c, 0)
    in_specs = [pl.BlockSpec((rows, conv_dim), tok),
                pl.BlockSpec((rows, d_inner), tok),
                pl.BlockSpec((rows, LANES), tok)]
    args = [xbc, z, dt]
    if has_init:
        in_specs += [pl.BlockSpec((None, 1) + conv_init.shape[2:], lambda b, c: (layer, b, 0, 0)),
                     pl.BlockSpec((None, 1) + ssm_init.shape[2:], lambda b, c: (layer, b, 0, 0, 0))]
        args += [conv_init, ssm_init]
    consts = [cw, cb, dtb, alog, dexp, nw, e]
    in_specs += [_const_spec(x.shape) for x in consts]
    args += consts
    aliases = {}
    if state_out is not None:
        aliases = {len(args): 1}
        in_specs.append(pl.BlockSpec(memory_space=pl.ANY))
        args.append(state_out)
    return pl.pallas_call(
        functools.partial(_ssd_scan_kernel, rows=rows, has_init=has_init, has_prev=state_out is not None,
                          heads=heads, headdim=headdim, groups=groups, state=state),
        grid=(nseq, nchunks),
        in_specs=in_specs,
        out_specs=[pl.BlockSpec((rows, d_inner), out_tok),
                   pl.BlockSpec((None, 1, heads, headdim, state), lambda b, c: (layer, b, 0, 0, 0))],
        out_shape=[jax.ShapeDtypeStruct((nseq * nchunks * rows, d_inner), F32),
                   jax.ShapeDtypeStruct((n_layers, nseq, heads, headdim, state), F32)],
        scratch_shapes=[pltpu.VMEM((SSD_CHUNK + 8 + 8, conv_dim), F32),
                        pltpu.VMEM((groups, state, (heads // groups) * headdim), F32)],
        input_output_aliases=aliases,
        compiler_params=_cparams("parallel", "arbitrary"),
        name="ssd_scan_init" if has_init else "ssd_scan",
    )(*args)


def _out_proj_kernel(*refs, final, n_first):
    if final:
        a0_ref, a1_ref, w_ref, res_ref, fw_ref, o_ref = refs
    else:
        a0_ref, a1_ref, w_ref, res_ref, o_ref = refs

    def run(a_ref):
        hnew = res_ref[...] + _dot(a_ref[...].astype(BF16), w_ref[...])
        if final:
            hnew = _rms(hnew, fw_ref[...])
        o_ref[...] = hnew

    i = pl.program_id(0)
    pl.when(i < n_first)(lambda: run(a0_ref))
    pl.when(i >= n_first)(lambda: run(a1_ref))


def _out_proj(a0, a1, w, res, final_w, tm):
    k = a0.shape[1]
    t = a0.shape[0] + a1.shape[0]
    n_first = a0.shape[0] // tm
    d = w.shape[1]
    final = final_w is not None
    in_specs = [pl.BlockSpec((tm, k), lambda i: (jnp.minimum(i, n_first - 1), 0)),
                pl.BlockSpec((tm, k), lambda i: (jnp.maximum(i - n_first, 0), 0)),
                _const_spec((k, d)), pl.BlockSpec((tm, d), lambda i: (i, 0))]
    args = [a0, a1, w, res]
    if final:
        in_specs.append(_const_spec((1, d)))
        args.append(final_w)
    return pl.pallas_call(
        functools.partial(_out_proj_kernel, final=final, n_first=n_first),
        grid=(t // tm,),
        in_specs=in_specs,
        out_specs=pl.BlockSpec((tm, d), lambda i: (i, 0)),
        out_shape=jax.ShapeDtypeStruct((t, d), F32),
        compiler_params=_cparams("parallel"),
        name="out_proj_final" if final else "out_proj",
    )(*args)


def _mla_proj_kernel(*refs, prompt, heads, q_lora, kv_lora, rope, nope, scale):
    if prompt:
        (x_ref, nw_ref, win_ref, qnw_ref, kvnw_ref, wq_ref, wqr_ref, cosq_ref, sinq_ref, cosk_ref, sink_ref,
         wk_ref, wvt_ref, ckv_ref, kpe_ref, gate_ref, q_ref, k_ref, vt_ref) = refs
    else:
        (x_ref, nw_ref, win_ref, qnw_ref, kvnw_ref, wq_ref, wqr_ref, cosq_ref, sinq_ref, cosk_ref, sink_ref,
         wukt_ref, ckv_ref, kpe_ref, gate_ref, qlat_ref, qpe_ref) = refs
    xn = _rms(x_ref[...], nw_ref[...]).astype(BF16)
    o1 = q_lora
    o2 = o1 + kv_lora
    cq = _rms(_dot(xn, win_ref[:, 0:o1]), qnw_ref[...]).astype(BF16)
    ckv = _rms(_dot(xn, win_ref[:, o1:o2]), kvnw_ref[...])
    ckv_ref[...] = ckv
    kp = _dot(xn, win_ref[:, o2:o2 + 2 * LANES])
    kpe = kp[:, :LANES] * cosk_ref[...] + kp[:, LANES:] * sink_ref[...]
    kpe_ref[...] = kpe[:, :rope]
    gate_ref[...] = _dot(xn, win_ref[:, o2 + 2 * LANES:])
    hc = 8
    cosq = jnp.concatenate([cosq_ref[...] * scale] * hc, axis=1)
    sinq = jnp.concatenate([sinq_ref[...] * scale] * hc, axis=1)
    for h0 in range(0, heads, hc):
        cs = slice(h0 * LANES, (h0 + hc) * LANES)
        qc = _dot(cq, wq_ref[:, cs]) * cosq + _dot(cq, wqr_ref[:, cs]) * sinq
        if prompt:
            q_ref[:, cs] = qc.astype(BF16)
        else:
            for h in range(h0, h0 + hc):
                off = (h - h0) * LANES
                qlat_ref[h] = _dot(qc[:, off:off + nope].astype(BF16), wukt_ref[h])
                qpe_ref[h] = qc[:, off + nope:off + nope + rope]
    if prompt:
        ckv16 = ckv.astype(BF16)
        kin = jnp.concatenate([ckv16, kpe.astype(BF16)], axis=1)
        for c0 in range(0, heads * LANES, 1024):
            k_ref[:, c0:c0 + 1024] = _dot(kin, wk_ref[:, c0:c0 + 1024]).astype(BF16)
        vt = _dot_nt(wvt_ref[...], ckv16)
        vhead = vt.shape[0] // heads
        npad = vt_ref.shape[1] // heads - vhead
        pad = (lax.broadcasted_iota(jnp.int32, (npad, vt.shape[1]), 0) == 0).astype(F32)
        pieces = []
        for h in range(heads):
            pieces += [vt[h * vhead:(h + 1) * vhead], pad]
        vt_ref[0] = jnp.concatenate(pieces, axis=0).astype(BF16)


def _mla_proj(h_all, row0_blocks, ntok, tm, seq_blocks, nw, w, tables, dims, prompt):
    heads, q_lora, kv_lora, rope, nope, vhead, scale = dims
    d = h_all.shape[1]
    cosq, sinq, cosk, sink = tables
    tok = lambda i: (row0_blocks + i, 0)
    out_tok = lambda i: (i, 0)
    pos = lambda i: (i % seq_blocks, 0)
    shared = [w["in"], w["qn"], w["kvn"], w["q"], w["qrot"]]
    in_specs = [pl.BlockSpec((tm, d), tok), _const_spec(nw.shape)] + [_const_spec(x.shape) for x in shared]
    in_specs += [pl.BlockSpec((tm, LANES), pos)] * 4
    args = [h_all, nw] + shared + [cosq, sinq, cosk, sink]
    gw = heads * vhead
    out_specs = [pl.BlockSpec((tm, kv_lora), out_tok), pl.BlockSpec((tm, rope), out_tok), pl.BlockSpec((tm, gw), out_tok)]
    out_shape = [jax.ShapeDtypeStruct((ntok, kv_lora), F32), jax.ShapeDtypeStruct((ntok, rope), F32),
                 jax.ShapeDtypeStruct((ntok, gw), F32)]
    if prompt:
        nb = ntok // (seq_blocks * tm)
        in_specs += [_const_spec(w["k"].shape), _const_spec(w["vt"].shape)]
        args += [w["k"], w["vt"]]
        vr = heads * _vt_rows(vhead)
        out_specs += [pl.BlockSpec((tm, heads * LANES), out_tok), pl.BlockSpec((tm, heads * LANES), out_tok),
                      pl.BlockSpec((1, vr, tm), lambda i: (i // seq_blocks, 0, i % seq_blocks))]
        out_shape += [jax.ShapeDtypeStruct((ntok, heads * LANES), BF16), jax.ShapeDtypeStruct((ntok, heads * LANES), BF16),
                      jax.ShapeDtypeStruct((nb, vr, seq_blocks * tm), BF16)]
    else:
        in_specs += [_const_spec(w["ukt"].shape)]
        args += [w["ukt"]]
        out_specs += [pl.BlockSpec((heads, tm, kv_lora), lambda i: (0, i, 0)),
                      pl.BlockSpec((heads, tm, rope), lambda i: (0, i, 0))]
        out_shape += [jax.ShapeDtypeStruct((heads, ntok, kv_lora), F32), jax.ShapeDtypeStruct((heads, ntok, rope), F32)]
    return pl.pallas_call(
        functools.partial(_mla_proj_kernel, prompt=prompt, heads=heads, q_lora=q_lora, kv_lora=kv_lora, rope=rope,
                          nope=nope, scale=scale),
        grid=(ntok // tm,),
        in_specs=in_specs,
        out_specs=out_specs,
        out_shape=out_shape,
        compiler_params=_cparams("parallel"),
        name="mla_proj_prompt" if prompt else "mla_proj_sample",
    )(*args)


def _vt_rows(vhead):
    bf16_sublanes = 16
    return -(-(vhead + 1) // bf16_sublanes) * bf16_sublanes


def _prompt_attn_kernel(qi_ref, ki_ref, q_ref, k_ref, vt_ref, gate_ref, o_ref, m_ref, acc_ref, *,
                        heads, vhead, tile):
    p = pl.program_id(1)
    qi = qi_ref[p]
    ki = ki_ref[p]
    vrows = _vt_rows(vhead)

    @pl.when(ki == 0)
    def _init():
        m_ref[...] = jnp.full(m_ref.shape, NEG_INF, F32)
        acc_ref[...] = jnp.zeros(acc_ref.shape, F32)

    def step(diagonal):
        if diagonal:
            visible = (lax.broadcasted_iota(jnp.int32, (tile, tile), 0)
                       <= lax.broadcasted_iota(jnp.int32, (tile, tile), 1))
        for h in range(heads):
            sl = slice(h * LANES, (h + 1) * LANES)
            st = _dot_nt(k_ref[:, sl], q_ref[:, sl])
            if diagonal:
                st = jnp.where(visible, st, NEG_INF)
            m_prev = m_ref[h]
            m_cur = jnp.maximum(m_prev, jnp.max(st, axis=0, keepdims=True))
            alpha = jnp.exp2(m_prev - m_cur)
            pt = jnp.exp2(st - m_cur).astype(BF16)
            acc_ref[h] = acc_ref[h] * alpha + _dot(vt_ref[0, h * vrows:(h + 1) * vrows, :], pt)
            m_ref[h] = m_cur

    pl.when(ki < qi)(lambda: step(False))

    @pl.when(ki == qi)
    def _last():
        step(True)
        ot = jnp.concatenate([acc_ref[h, :vhead, :] / acc_ref[h, vhead:vhead + 1, :] for h in range(heads)], axis=0)
        o_ref[...] = ot.T * _silu(gate_ref[...])


def _prompt_attn(q, k, vt, gate, nb, seq, heads, vhead, tile):
    nq = seq // tile
    pairs = [(a, b) for a in range(nq) for b in range(a + 1)]
    qi_tab = jnp.asarray([a for a, _ in pairs], jnp.int32)
    ki_tab = jnp.asarray([b for _, b in pairs], jnp.int32)
    vrows = _vt_rows(vhead)
    grid_spec = pltpu.PrefetchScalarGridSpec(
        num_scalar_prefetch=2,
        grid=(nb, len(pairs)),
        in_specs=[pl.BlockSpec((tile, heads * LANES), lambda b, p, qt, kt: (b * nq + qt[p], 0)),
                  pl.BlockSpec((tile, heads * LANES), lambda b, p, qt, kt: (b * nq + kt[p], 0)),
                  pl.BlockSpec((1, heads * vrows, tile), lambda b, p, qt, kt: (b, 0, kt[p])),
                  pl.BlockSpec((tile, heads * vhead), lambda b, p, qt, kt: (b * nq + qt[p], 0))],
        out_specs=pl.BlockSpec((tile, heads * vhead), lambda b, p, qt, kt: (b * nq + qt[p], 0)),
        scratch_shapes=[pltpu.VMEM((heads, 1, tile), F32),
                        pltpu.VMEM((heads, vrows, tile), F32)])
    return pl.pallas_call(
        functools.partial(_prompt_attn_kernel, heads=heads, vhead=vhead, tile=tile),
        grid_spec=grid_spec,
        out_shape=jax.ShapeDtypeStruct((nb * seq, heads * vhead), F32),
        compiler_params=_cparams("parallel", "arbitrary"),
        name="mla_prompt_attn",
    )(qi_tab, ki_tab, q, k, vt, gate)


def _sample_attn_kernel(pt_ref, qlat_ref, qpe_ref, ckvn_ref, kpen_ref, *refs, npages, heads, tnew):
    ckv_refs = refs[:npages]
    kpe_refs = refs[npages:2 * npages]
    o_ref, m_ref, l_ref, acc_ref = refs[2 * npages:]
    s = pl.program_id(1)
    rows = heads * tnew
    q = qlat_ref[...].reshape(rows, qlat_ref.shape[2]).astype(BF16)
    qp = qpe_ref[...].reshape(rows, qpe_ref.shape[2]).astype(BF16)

    @pl.when(s == 0)
    def _new_tokens():
        cn = ckvn_ref[...].astype(BF16)
        sc = _dot_nt(q, cn) + _dot_nt(qp, kpen_ref[...].astype(BF16))
        tq_ = lax.broadcasted_iota(jnp.int32, (rows, tnew), 0) % tnew
        tk_ = lax.broadcasted_iota(jnp.int32, (rows, tnew), 1)
        sc = jnp.where(tk_ <= tq_, sc, NEG_INF)
        m = jnp.max(sc, axis=-1, keepdims=True)
        pr = jnp.exp2(sc - m)
        m_ref[...] = m
        l_ref[...] = jnp.sum(pr, axis=-1, keepdims=True)
        acc_ref[...] = _dot(pr, ckvn_ref[...])

    ck = jnp.concatenate([r[...].astype(BF16) for r in ckv_refs], axis=0)
    kp = jnp.concatenate([r[...].astype(BF16) for r in kpe_refs], axis=1)
    sc = _dot_nt(q, ck) + _dot(qp, kp)
    m_prev = m_ref[...]
    m_cur = jnp.maximum(m_prev, jnp.max(sc, axis=-1, keepdims=True))
    alpha = jnp.exp2(m_prev - m_cur)
    pr = jnp.exp2(sc - m_cur)
    l_ref[...] = alpha * l_ref[...] + jnp.sum(pr, axis=-1, keepdims=True)
    acc_ref[...] = acc_ref[...] * alpha + _dot(pr.astype(BF16), ck)
    m_ref[...] = m_cur

    @pl.when(s == pl.num_programs(1) - 1)
    def _fin():
        o_ref[0] = acc_ref[...] / l_ref[...]


def _sample_attn(page_table, qlat, qpe, ckv_new, kpe_new, cache_ckv, cache_kpe_t, layer, npages):
    heads, ntok, kv_lora = qlat.shape
    rope = qpe.shape[2]
    nb, n_pages_total = page_table.shape
    tnew = ntok // nb
    page = cache_ckv.shape[2]
    steps = n_pages_total // npages
    rows = heads * tnew

    def page_map(i):
        return lambda b, s, pt: (layer, pt[b, s * npages + i], 0, 0)

    in_specs = [pl.BlockSpec((heads, tnew, kv_lora), lambda b, s, pt: (0, b, 0)),
                pl.BlockSpec((heads, tnew, rope), lambda b, s, pt: (0, b, 0)),
                pl.BlockSpec((tnew, kv_lora), lambda b, s, pt: (b, 0)),
                pl.BlockSpec((tnew, rope), lambda b, s, pt: (b, 0))]
    in_specs += [pl.BlockSpec((None, None, page, kv_lora), page_map(i)) for i in range(npages)]
    in_specs += [pl.BlockSpec((None, None, rope, page), page_map(i)) for i in range(npages)]
    grid_spec = pltpu.PrefetchScalarGridSpec(
        num_scalar_prefetch=1,
        grid=(nb, steps),
        in_specs=in_specs,
        out_specs=pl.BlockSpec((1, rows, kv_lora), lambda b, s, pt: (b, 0, 0)),
        scratch_shapes=[pltpu.VMEM((rows, 1), F32), pltpu.VMEM((rows, 1), F32), pltpu.VMEM((rows, kv_lora), F32)])
    return pl.pallas_call(
        functools.partial(_sample_attn_kernel, npages=npages, heads=heads, tnew=tnew),
        grid_spec=grid_spec,
        out_shape=jax.ShapeDtypeStruct((nb, rows, kv_lora), F32),
        compiler_params=_cparams("parallel", "arbitrary"),
        name="mla_sample_attn",
    )(page_table, qlat, qpe, ckv_new, kpe_new, *([cache_ckv] * npages), *([cache_kpe_t] * npages))


def _sample_post_kernel(ol_ref, wuv_ref, gate_ref, o_ref, *, heads, tnew, vhead):
    bt = ol_ref.shape[0]
    for h in range(heads):
        x = ol_ref[:, h * tnew:(h + 1) * tnew, :].reshape(bt * tnew, ol_ref.shape[2]).astype(BF16)
        g = gate_ref[:, h * vhead:(h + 1) * vhead]
        o_ref[:, h * vhead:(h + 1) * vhead] = _dot(x, wuv_ref[h]) * _silu(g)


def _sample_post(olat, wuv, gate, tnew, bt):
    nb, rows, kv_lora = olat.shape
    heads, _, vhead = wuv.shape
    ntok = nb * tnew
    return pl.pallas_call(
        functools.partial(_sample_post_kernel, heads=heads, tnew=tnew, vhead=vhead),
        grid=(nb // bt,),
        in_specs=[pl.BlockSpec((bt, rows, kv_lora), lambda i: (i, 0, 0)),
                  _const_spec(wuv.shape),
                  pl.BlockSpec((bt * tnew, heads * vhead), lambda i: (i, 0))],
        out_specs=pl.BlockSpec((bt * tnew, heads * vhead), lambda i: (i, 0)),
        out_shape=jax.ShapeDtypeStruct((ntok, heads * vhead), F32),
        compiler_params=_cparams("parallel"),
        name="mla_sample_post",
    )(olat, wuv, gate)


def _rope_tables(pos, rope, nope):
    half = rope // 2
    inv = 1.0 / (ROPE_BASE ** (jnp.arange(0, rope, 2, dtype=F32) / rope))
    ang = pos[:, None] * inv[None, :]
    cos, sin = jnp.cos(ang), jnp.sin(ang)
    n = pos.shape[0]
    cc = jnp.concatenate([cos, cos], axis=1)
    ss = jnp.concatenate([sin, sin], axis=1)
    cosk = jnp.pad(cc, ((0, 0), (0, LANES - rope)))
    sink = jnp.pad(ss, ((0, 0), (0, LANES - rope)))
    cosq = jnp.concatenate([jnp.ones((n, nope), F32), cc, jnp.zeros((n, LANES - nope - rope), F32)], axis=1)
    sinq = jnp.concatenate([jnp.zeros((n, nope), F32), ss, jnp.zeros((n, LANES - nope - rope), F32)], axis=1)
    del half
    return cosq, sinq, cosk, sink


def _rot_cols(w, half):
    return jnp.concatenate([-w[..., half:], w[..., :half]], axis=-1)


def _mla_weights(w_in, qn, kvn, w_uq, w_uk, w_uv, q_lora, kv_lora, rope, nope):
    d = w_in.shape[0]
    heads = w_uq.shape[1]
    vhead = w_uv.shape[2]
    o1, o2, o3 = q_lora, q_lora + kv_lora, q_lora + kv_lora + rope
    pad = LANES - rope
    wkpe = w_in[:, o2:o3]
    w_in_big = jnp.concatenate([w_in[:, :o2], jnp.pad(wkpe, ((0, 0), (0, pad))),
                                jnp.pad(_rot_cols(wkpe, rope // 2), ((0, 0), (0, pad))), w_in[:, o3:]], axis=1)
    q_nope, q_pe = w_uq[:, :, :nope], w_uq[:, :, nope:]
    zpad = jnp.zeros((q_lora, heads, LANES - nope - rope), F32)
    wq = jnp.concatenate([q_nope, q_pe, zpad], axis=2).reshape(q_lora, heads * LANES)
    wqr = jnp.concatenate([jnp.zeros_like(q_nope), _rot_cols(q_pe, rope // 2), zpad], axis=2).reshape(q_lora, heads * LANES)
    wk_lat = jnp.concatenate([w_uk, jnp.zeros((kv_lora, heads, LANES - nope), F32)], axis=2).reshape(kv_lora, heads * LANES)
    place = jnp.zeros((LANES, heads, LANES), F32).at[jnp.arange(rope), :, nope + jnp.arange(rope)].set(1.0)
    wk = jnp.concatenate([wk_lat, place.reshape(LANES, heads * LANES)], axis=0)
    wvt = w_uv.transpose(1, 2, 0).reshape(heads * vhead, kv_lora)
    return {"in": w_in_big.astype(BF16), "qn": qn.reshape(1, -1), "kvn": kvn.reshape(1, -1),
            "q": wq.astype(BF16), "qrot": wqr.astype(BF16), "k": wk.astype(BF16), "vt": wvt.astype(BF16),
            "ukt": w_uk.transpose(1, 2, 0).astype(BF16), "uv": w_uv.transpose(1, 0, 2).astype(BF16)}


def kernel(x_prompt, x_sample, state_ssm, state_conv, cache_ckv, cache_kpe, page_table, norm_w, final_norm_w,
           ssd_w_in, ssd_conv_w, ssd_conv_b, ssd_dt_bias, ssd_a_log, ssd_d, ssd_norm_w, ssd_w_out,
           mla_w_in, mla_q_norm_w, mla_kv_norm_w, mla_w_uq, mla_w_uk, mla_w_uv, mla_w_out):
    nb, seq, d = x_prompt.shape
    db, dseq, _ = x_sample.shape
    depth = norm_w.shape[0]
    tp, ts = nb * seq, db * dseq
    _, _, heads_s, headdim, state = state_ssm.shape
    d_inner = heads_s * headdim
    conv_dim = ssd_conv_w.shape[2]
    gn = (conv_dim - d_inner) // 2
    groups = gn // state
    q_lora, heads_m, qk = mla_w_uq.shape[1:]
    kv_lora = mla_w_uk.shape[1]
    nope = mla_w_uk.shape[3]
    rope = qk - nope
    vhead = mla_w_uv.shape[3]
    past = page_table.shape[1] * cache_ckv.shape[2]
    scale = (nope + rope) ** -0.5 * LOG2E

    assert seq % SSD_CHUNK == 0 and dseq <= SSD_CHUNK and tp % dseq == 0
    tm = 512
    assert tp % tm == 0 and ts % tm == 0

    h_all = jnp.concatenate([x_prompt.reshape(tp, d), x_sample.reshape(ts, d)], axis=0)

    e = (jnp.arange(d_inner)[None, :] // headdim == jnp.arange(LANES)[:, None]).astype(BF16)
    tab_p = _rope_tables(jnp.arange(seq, dtype=F32), rope, nope)
    tab_s = _rope_tables(jnp.tile(past + jnp.arange(dseq, dtype=F32), ts // dseq), rope, nope)
    mla_dims = (heads_m, q_lora, kv_lora, rope, nope, vhead, scale)

    cache_kpe_t = jnp.swapaxes(cache_kpe, 2, 3)
    n_ssd = (depth + 1) // 2
    tail = SSD_CONV_W - 1

    p_conv, p_ckv, p_kpe = [], [], []
    s_conv, s_ckv, s_kpe = [], [], []
    p_ssm = s_ssm = None
    for i in range(depth):
        j = i // 2
        nw = norm_w[i].reshape(1, d)
        fin_w = final_norm_w.reshape(1, d) if i == depth - 1 else None
        if i % 2 == 0:
            w_big = jnp.pad(ssd_w_in[j], ((0, 0), (0, LANES - heads_s))).astype(BF16)
            z, xbc, dt = _ssd_in_proj(h_all, nw, w_big, d_inner, conv_dim, tm)
            params = (ssd_conv_w[j], ssd_conv_b[j].reshape(1, -1),
                      jnp.pad(ssd_dt_bias[j], (0, LANES - heads_s)).reshape(1, LANES),
                      jnp.pad(ssd_a_log[j], (0, LANES - heads_s)).reshape(1, LANES),
                      jnp.repeat(ssd_d[j], headdim).reshape(1, d_inner),
                      ssd_norm_w[j].reshape(1, d_inner), e, (heads_s, headdim, groups, state))
            y_p, p_ssm = _ssd_scan(xbc, z, dt, 0, nb, seq // SSD_CHUNK, SSD_CHUNK, params, j, n_ssd,
                                   None, None, p_ssm)
            y_s, s_ssm = _ssd_scan(xbc, z, dt, tp // dseq, db, 1, dseq, params, j, n_ssd,
                                   state_conv, state_ssm, s_ssm)
            h_all = _out_proj(y_p, y_s, ssd_w_out[j].astype(BF16), h_all, fin_w, tm)
            p_conv.append(jnp.stack([xbc[(b + 1) * seq - tail:(b + 1) * seq] for b in range(nb)]))
            s_conv.append(xbc[tp:].reshape(db, dseq, conv_dim)[:, dseq - tail:])
        else:
            w = _mla_weights(mla_w_in[j], mla_q_norm_w[j], mla_kv_norm_w[j], mla_w_uq[j], mla_w_uk[j], mla_w_uv[j],
                             q_lora, kv_lora, rope, nope)
            tmp = 256
            ckv_p, kpe_p, gate_p, q_p, k_p, vt_p = _mla_proj(h_all, 0, tp, tmp, seq // tmp, nw, w, tab_p, mla_dims, True)
            ckv_s, kpe_s, gate_s, qlat_s, qpe_s = _mla_proj(h_all, tp // tmp, ts, tmp, ts // tmp, nw, w, tab_s,
                                                            mla_dims, False)
            og_p = _prompt_attn(q_p, k_p, vt_p, gate_p, nb, seq, heads_m, vhead, math.gcd(ATTN_TILE, seq))
            npg = math.gcd(PAGES_PER_STEP, page_table.shape[1])
            olat = _sample_attn(page_table, qlat_s, qpe_s, ckv_s, kpe_s, cache_ckv, cache_kpe_t, j, npg)
            og_s = _sample_post(olat, w["uv"], gate_s, dseq, math.gcd(16, db))
            h_all = _out_proj(og_p, og_s, mla_w_out[j].astype(BF16), h_all, fin_w, tm)
            p_ckv.append(ckv_p.reshape(nb, seq, kv_lora))
            p_kpe.append(kpe_p.reshape(nb, seq, rope))
            s_ckv.append(ckv_s.reshape(db, dseq, kv_lora))
            s_kpe.append(kpe_s.reshape(db, dseq, rope))
    y_prompt = h_all[:tp].reshape(nb, seq, d)
    y_sample = h_all[tp:].reshape(db, dseq, d)
    return (y_prompt, y_sample,
            p_ssm, jnp.stack(p_conv), jnp.stack(p_ckv), jnp.stack(p_kpe),
            s_ssm, jnp.stack(s_conv), jnp.stack(s_ckv), jnp.stack(s_kpe))
```

```python
import functools
import math

import jax
import jax.numpy as jnp
from jax import lax
from jax.experimental import pallas as pl
from jax.experimental.pallas import tpu as pltpu

F32 = jnp.float32
BF16 = jnp.bfloat16

NORM_EPS = 1e-6
NEG_INF = -1e30
ROPE_BASE = 10000.0
LOG2E = 1.4426950408889634

LANES = 128
VMEM_LIMIT_BYTES = 56 * 1024 * 1024

SSD_CHUNK = 128
SSD_CONV_W = 4
PAGES_PER_STEP = 16
ATTN_TILE = 512
DECODE_CHAINS = 2
SCORE_AHEAD = 1


def _cparams(*sem):
    return pltpu.CompilerParams(dimension_semantics=sem, vmem_limit_bytes=VMEM_LIMIT_BYTES)


def _const_spec(shape):
    nd = len(shape)
    return pl.BlockSpec(shape, lambda *_: (0,) * nd, pipeline_mode=pl.Buffered(1))


def _sigmoid(x):
    return 1.0 / (1.0 + jnp.exp(-x))


def _silu(x):
    return x * _sigmoid(x)


def _dot(a, b):
    return jnp.dot(a, b, preferred_element_type=F32)


def _dot_nt(a, b):
    return lax.dot_general(a, b, (((1,), (1,)), ((), ())), preferred_element_type=F32)


def _rms(x, w):
    return x * lax.rsqrt(jnp.mean(x * x, axis=-1, keepdims=True) + NORM_EPS) * w


def _ssd_in_kernel(x_ref, nw_ref, w_ref, z_ref, xbc_ref, dt_ref, *, d_inner, conv_dim, n_chunk):
    xn = _rms(x_ref[...], nw_ref[...]).astype(BF16)
    col = 0
    for o_ref, width in ((z_ref, d_inner), (xbc_ref, conv_dim), (dt_ref, LANES)):
        for c0 in range(0, width, n_chunk):
            c1 = min(c0 + n_chunk, width)
            o_ref[:, c0:c1] = _dot(xn, w_ref[:, col + c0:col + c1]).astype(o_ref.dtype)
        col += width


def _ssd_in_proj(h, norm_w, w_big, d_inner, conv_dim, tm):
    t, d = h.shape
    n = w_big.shape[1]
    return pl.pallas_call(
        functools.partial(_ssd_in_kernel, d_inner=d_inner, conv_dim=conv_dim, n_chunk=1024),
        grid=(t // tm,),
        in_specs=[pl.BlockSpec((tm, d), lambda i: (i, 0)),
                  _const_spec((1, d)),
                  _const_spec((d, n))],
        out_specs=[pl.BlockSpec((tm, d_inner), lambda i: (i, 0)),
                   pl.BlockSpec((tm, conv_dim), lambda i: (i, 0)),
                   pl.BlockSpec((tm, LANES), lambda i: (i, 0))],
        out_shape=[jax.ShapeDtypeStruct((t, d_inner), F32),
                   jax.ShapeDtypeStruct((t, conv_dim), F32),
                   jax.ShapeDtypeStruct((t, LANES), F32)],
        compiler_params=_cparams("parallel"),
        name="ssd_in_proj",
    )(h, norm_w, w_big)


def _ssd_scan_kernel(*refs, rows, has_init, has_prev, layer, heads, headdim, groups, state):
    if has_prev:
        refs = refs[:-5] + refs[-4:]
    if has_init:
        (xbc_ref, z_ref, dt_ref, cinit_ref, sinit_ref, cw_ref, cb_ref, dtb_ref, alog_ref, dexp_ref,
         nw_ref, e_ref, y_ref, sout_ref, win_ref, st_ref) = refs
    else:
        (xbc_ref, z_ref, dt_ref, cw_ref, cb_ref, dtb_ref, alog_ref, dexp_ref,
         nw_ref, e_ref, y_ref, sout_ref, win_ref, st_ref) = refs
    L = SSD_CHUNK
    d_inner = heads * headdim
    hpg = heads // groups
    gw = hpg * headdim
    gn = groups * state
    c = pl.program_id(1)
    nc = pl.num_programs(1)

    @pl.when(c == 0)
    def _init():
        win_ref[0:8, :] = jnp.zeros((8, win_ref.shape[1]), F32)
        if has_init:
            win_ref[8 - (SSD_CONV_W - 1):8, :] = cinit_ref[0]
            for g in range(groups):
                sg = sinit_ref[0, g * hpg:(g + 1) * hpg].reshape(gw, state)
                st_ref[g] = sg.T
        else:
            st_ref[...] = jnp.zeros(st_ref.shape, F32)

    R = rows

    def pad_time(v):
        if R == L:
            return v
        return jnp.concatenate([v, jnp.zeros((L - R,) + v.shape[1:], v.dtype)], axis=0)

    win_ref[8:8 + R, :] = xbc_ref[...]

    base = 8 - (SSD_CONV_W - 1)
    conv = cb_ref[...] + cw_ref[0:1, :] * win_ref[base:base + R, :]
    for k in range(1, SSD_CONV_W):
        conv = conv + cw_ref[k:k + 1, :] * win_ref[base + k:base + k + R, :]
    if R == L:
        win_ref[0:8, :] = win_ref[L:L + 8, :]
    xc = _silu(conv)

    dt = dt_ref[...] + dtb_ref[...]
    dt = jnp.maximum(dt, 0.0) + jnp.log(1.0 + jnp.exp(-jnp.abs(dt)))
    a = pad_time(dt * (-jnp.exp(alog_ref[...])))

    ri = lax.broadcasted_iota(jnp.int32, (L, L), 0)
    ci = lax.broadcasted_iota(jnp.int32, (L, L), 1)
    causal_full = ri >= ci
    tril = causal_full.astype(F32)
    a_cs_full = jnp.dot(tril, a, precision=lax.Precision.HIGHEST, preferred_element_type=F32)
    a_cst = a_cs_full.T
    a_cs = a_cs_full[:R]
    a_last = a_cs_full[L - 1:L, :]
    causal = causal_full[:R]

    e = e_ref[...]

    def expand(v):
        hi = v.astype(BF16)
        lo = (v - hi.astype(F32)).astype(BF16)
        return _dot(hi, e) + _dot(lo, e)

    dt_x = expand(dt)
    din_x = expand(jnp.exp(a_cs))
    dend_x = expand(jnp.exp(a_last - a_cs))
    cdec_x = din_x[R - 1:R, :]

    rb = lax.broadcasted_iota(jnp.int32, (hpg * L, gw), 0) // L
    cbk = lax.broadcasted_iota(jnp.int32, (hpg * L, gw), 1) // headdim
    blockdiag = rb == cbk

    def group_matmuls(g):
        bg = pad_time(xc[:, d_inner + g * state:d_inner + (g + 1) * state])
        cg16 = xc[:, d_inner + gn + g * state:d_inner + gn + (g + 1) * state].astype(BF16)
        st = st_ref[g]
        return bg, _dot_nt(cg16, bg.astype(BF16)), st, _dot(cg16, st.astype(BF16))

    ahead = group_matmuls(0)
    for g in range(groups):
        sl = slice(g * gw, (g + 1) * gw)
        bg, gmat, st, y_off = ahead
        if g + 1 < groups:
            ahead = group_matmuls(g + 1)
        xg = xc[:, sl]
        ms = []
        for hh in range(hpg):
            hd = g * hpg + hh
            seg = a_cs[:, hd:hd + 1] - a_cst[hd:hd + 1, :]
            decay = jnp.where(causal, jnp.exp(jnp.where(causal, seg, 0.0)), 0.0)
            ms.append((gmat * decay).astype(BF16))
        mcat = jnp.concatenate(ms, axis=1)
        xdt = xg * dt_x[:, sl]
        xdt16 = pad_time(xdt).astype(BF16)
        bd = jnp.where(blockdiag, jnp.concatenate([xdt16] * hpg, axis=0), jnp.zeros((), BF16))
        y = _dot(mcat, bd) + y_off * din_x[:, sl] + dexp_ref[:, sl] * xg
        xd = pad_time(xdt * dend_x[:, sl]).astype(BF16)
        st_ref[g] = st * cdec_x[:, sl] + _dot(bg.T.astype(BF16), xd)
        yz = y * _silu(z_ref[:, sl])
        yn = yz * lax.rsqrt(jnp.mean(yz * yz, axis=-1, keepdims=True) + NORM_EPS) * nw_ref[:, sl]
        y_ref[:, sl] = yn.astype(y_ref.dtype)

    @pl.when(c == nc - 1)
    def _fin():
        if has_prev:
            out = sout_ref.at[0]
        else:
            out = sout_ref.at[layer, 0]
            for other in range(sout_ref.shape[0]):
                if other != layer:
                    sout_ref[other] = jnp.zeros(sout_ref.shape[1:], F32)
        for g in range(groups):
            out[g * hpg:(g + 1) * hpg] = st_ref[g].T.reshape(hpg, headdim, state)


def _ssd_scan(xbc, z, dt, row0, nseq, nchunks, rows, params, layer, n_layers, conv_init, ssm_init, state_out):
    cw, cb, dtb, alog, dexp, nw, e, dims = params
    heads, headdim, groups, state = dims
    d_inner = heads * headdim
    conv_dim = xbc.shape[1]
    has_init = conv_init is not None
    assert rows == SSD_CHUNK or nchunks == 1
    tok = lambda b, c: (row0 + b * nchunks + c, 0)
    out_tok = lambda b, c: (b * nchunks + c, 0)
    in_specs = [pl.BlockSpec((rows, conv_dim), tok),
                pl.BlockSpec((rows, d_inner), tok),
                pl.BlockSpec((rows, LANES), tok)]
    args = [xbc, z, dt]
    if has_init:
        in_specs += [pl.BlockSpec((None, 1) + conv_init.shape[2:], lambda b, c: (layer, b, 0, 0)),
                     pl.BlockSpec((None, 1) + ssm_init.shape[2:], lambda b, c: (layer, b, 0, 0, 0))]
        args += [conv_init, ssm_init]
    consts = [cw, cb, dtb, alog, dexp, nw, e]
    in_specs += [_const_spec(x.shape) for x in consts]
    args += consts
    aliases = {}
    if state_out is not None:
        aliases = {len(args): 1}
        in_specs.append(pl.BlockSpec(memory_space=pl.ANY))
        args.append(state_out)
        state_spec = pl.BlockSpec((None, 1, heads, headdim, state), lambda b, c: (layer, b, 0, 0, 0))
    else:
        state_spec = pl.BlockSpec((n_layers, 1, heads, headdim, state), lambda b, c: (0, b, 0, 0, 0))
    return pl.pallas_call(
        functools.partial(_ssd_scan_kernel, rows=rows, has_init=has_init, has_prev=state_out is not None,
                          layer=layer, heads=heads, headdim=headdim, groups=groups, state=state),
        grid=(nseq, nchunks),
        in_specs=in_specs,
        out_specs=[pl.BlockSpec((rows, d_inner), out_tok), state_spec],
        out_shape=[jax.ShapeDtypeStruct((nseq * nchunks * rows, d_inner), F32),
                   jax.ShapeDtypeStruct((n_layers, nseq, heads, headdim, state), F32)],
        scratch_shapes=[pltpu.VMEM((SSD_CHUNK + 8 + 8, conv_dim), F32),
                        pltpu.VMEM((groups, state, (heads // groups) * headdim), F32)],
        input_output_aliases=aliases,
        compiler_params=_cparams("parallel", "arbitrary"),
        name="ssd_scan_init" if has_init else "ssd_scan",
    )(*args)


def _out_proj_kernel(*refs, final, n_first):
    if final:
        a0_ref, a1_ref, w_ref, res_ref, fw_ref, o_ref = refs
    else:
        a0_ref, a1_ref, w_ref, res_ref, o_ref = refs

    def run(a_ref):
        hnew = res_ref[...] + _dot(a_ref[...].astype(BF16), w_ref[...])
        if final:
            hnew = _rms(hnew, fw_ref[...])
        o_ref[...] = hnew

    i = pl.program_id(0)
    pl.when(i < n_first)(lambda: run(a0_ref))
    pl.when(i >= n_first)(lambda: run(a1_ref))


def _out_proj(a0, a1, w, res, final_w, tm):
    k = a0.shape[1]
    t = a0.shape[0] + a1.shape[0]
    n_first = a0.shape[0] // tm
    d = w.shape[1]
    final = final_w is not None
    in_specs = [pl.BlockSpec((tm, k), lambda i: (jnp.minimum(i, n_first - 1), 0)),
                pl.BlockSpec((tm, k), lambda i: (jnp.maximum(i - n_first, 0), 0)),
                _const_spec((k, d)), pl.BlockSpec((tm, d), lambda i: (i, 0))]
    args = [a0, a1, w, res]
    if final:
        in_specs.append(_const_spec((1, d)))
        args.append(final_w)
    return pl.pallas_call(
        functools.partial(_out_proj_kernel, final=final, n_first=n_first),
        grid=(t // tm,),
        in_specs=in_specs,
        out_specs=pl.BlockSpec((tm, d), lambda i: (i, 0)),
        out_shape=jax.ShapeDtypeStruct((t, d), F32),
        compiler_params=_cparams("parallel"),
        name="out_proj_final" if final else "out_proj",
    )(*args)


def _mla_proj_kernel(*refs, prompt, heads, q_lora, kv_lora, rope, nope, scale):
    if prompt:
        (x_ref, nw_ref, win_ref, qnw_ref, kvnw_ref, wq_ref, wqr_ref, cosq_ref, sinq_ref, cosk_ref, sink_ref,
         wk_ref, wvt_ref, ckv_ref, kpe_ref, gate_ref, q_ref, k_ref, vt_ref) = refs
    else:
        (x_ref, nw_ref, win_ref, qnw_ref, kvnw_ref, wq_ref, wqr_ref, cosq_ref, sinq_ref, cosk_ref, sink_ref,
         wukt_ref, ckv_ref, kpe_ref, gate_ref, qlat_ref, qpe_ref) = refs
    xn = _rms(x_ref[...], nw_ref[...]).astype(BF16)
    o1 = q_lora
    o2 = o1 + kv_lora
    cq = _rms(_dot(xn, win_ref[:, 0:o1]), qnw_ref[...]).astype(BF16)
    ckv = _rms(_dot(xn, win_ref[:, o1:o2]), kvnw_ref[...])
    ckv_ref[...] = ckv
    kp = _dot(xn, win_ref[:, o2:o2 + 2 * LANES])
    kpe = kp[:, :LANES] * cosk_ref[...] + kp[:, LANES:] * sink_ref[...]
    kpe_ref[...] = kpe[:, :rope]
    gate_ref[...] = _dot(xn, win_ref[:, o2 + 2 * LANES:])
    hc = 8
    cosq = jnp.concatenate([cosq_ref[...] * scale] * hc, axis=1)
    sinq = jnp.concatenate([sinq_ref[...] * scale] * hc, axis=1)
    for h0 in range(0, heads, hc):
        cs = slice(h0 * LANES, (h0 + hc) * LANES)
        qc = _dot(cq, wq_ref[:, cs]) * cosq + _dot(cq, wqr_ref[:, cs]) * sinq
        if prompt:
            q_ref[:, cs] = qc.astype(BF16)
        else:
            for h in range(h0, h0 + hc):
                off = (h - h0) * LANES
                qlat_ref[h] = _dot(qc[:, off:off + nope].astype(BF16), wukt_ref[h])
                qpe_ref[h] = qc[:, off + nope:off + nope + rope]
    if prompt:
        ckv16 = ckv.astype(BF16)
        kin = jnp.concatenate([ckv16, kpe.astype(BF16)], axis=1)
        for c0 in range(0, heads * LANES, 1024):
            k_ref[:, c0:c0 + 1024] = _dot(kin, wk_ref[:, c0:c0 + 1024]).astype(BF16)
        vt = _dot_nt(wvt_ref[...], ckv16)
        vhead = vt.shape[0] // heads
        npad = vt_ref.shape[1] // heads - vhead
        pad = (lax.broadcasted_iota(jnp.int32, (npad, vt.shape[1]), 0) == 0).astype(F32)
        pieces = []
        for h in range(heads):
            pieces += [vt[h * vhead:(h + 1) * vhead], pad]
        vt_ref[0] = jnp.concatenate(pieces, axis=0).astype(BF16)


def _mla_proj(h_all, row0_blocks, ntok, tm, seq_blocks, nw, w, tables, dims, prompt):
    heads, q_lora, kv_lora, rope, nope, vhead, scale = dims
    d = h_all.shape[1]
    cosq, sinq, cosk, sink = tables
    tok = lambda i: (row0_blocks + i, 0)
    out_tok = lambda i: (i, 0)
    pos = lambda i: (i % seq_blocks, 0)
    shared = [w["in"], w["qn"], w["kvn"], w["q"], w["qrot"]]
    in_specs = [pl.BlockSpec((tm, d), tok), _const_spec(nw.shape)] + [_const_spec(x.shape) for x in shared]
    in_specs += [pl.BlockSpec((tm, LANES), pos)] * 4
    args = [h_all, nw] + shared + [cosq, sinq, cosk, sink]
    gw = heads * vhead
    out_specs = [pl.BlockSpec((tm, kv_lora), out_tok), pl.BlockSpec((tm, rope), out_tok), pl.BlockSpec((tm, gw), out_tok)]
    out_shape = [jax.ShapeDtypeStruct((ntok, kv_lora), F32), jax.ShapeDtypeStruct((ntok, rope), F32),
                 jax.ShapeDtypeStruct((ntok, gw), F32)]
    if prompt:
        nb = ntok // (seq_blocks * tm)
        in_specs += [_const_spec(w["k"].shape), _const_spec(w["vt"].shape)]
        args += [w["k"], w["vt"]]
        vr = heads * _vt_rows(vhead)
        out_specs += [pl.BlockSpec((tm, heads * LANES), out_tok), pl.BlockSpec((tm, heads * LANES), out_tok),
                      pl.BlockSpec((1, vr, tm), lambda i: (i // seq_blocks, 0, i % seq_blocks))]
        out_shape += [jax.ShapeDtypeStruct((ntok, heads * LANES), BF16), jax.ShapeDtypeStruct((ntok, heads * LANES), BF16),
                      jax.ShapeDtypeStruct((nb, vr, seq_blocks * tm), BF16)]
    else:
        in_specs += [_const_spec(w["ukt"].shape)]
        args += [w["ukt"]]
        out_specs += [pl.BlockSpec((heads, tm, kv_lora), lambda i: (0, i, 0)),
                      pl.BlockSpec((heads, tm, rope), lambda i: (0, i, 0))]
        out_shape += [jax.ShapeDtypeStruct((heads, ntok, kv_lora), F32), jax.ShapeDtypeStruct((heads, ntok, rope), F32)]
    return pl.pallas_call(
        functools.partial(_mla_proj_kernel, prompt=prompt, heads=heads, q_lora=q_lora, kv_lora=kv_lora, rope=rope,
                          nope=nope, scale=scale),
        grid=(ntok // tm,),
        in_specs=in_specs,
        out_specs=out_specs,
        out_shape=out_shape,
        compiler_params=_cparams("parallel"),
        name="mla_proj_prompt" if prompt else "mla_proj_sample",
    )(*args)


def _vt_rows(vhead):
    bf16_sublanes = 16
    return -(-(vhead + 1) // bf16_sublanes) * bf16_sublanes


def _prompt_attn_kernel(qi_ref, ki_ref, q_ref, k_ref, vt_ref, gate_ref, o_ref, m_ref, acc_ref, *,
                        heads, vhead, tile):
    p = pl.program_id(1)
    qi = qi_ref[p]
    ki = ki_ref[p]
    vrows = _vt_rows(vhead)

    @pl.when(ki == 0)
    def _init():
        m_ref[...] = jnp.full(m_ref.shape, NEG_INF, F32)
        acc_ref[...] = jnp.zeros(acc_ref.shape, F32)

    def step(diagonal):
        if diagonal:
            visible = (lax.broadcasted_iota(jnp.int32, (tile, tile), 0)
                       <= lax.broadcasted_iota(jnp.int32, (tile, tile), 1))

        def scores(h):
            sl = slice(h * LANES, (h + 1) * LANES)
            st = _dot_nt(k_ref[:, sl], q_ref[:, sl])
            return jnp.where(visible, st, NEG_INF) if diagonal else st

        def new_max(h, st):
            m_prev = m_ref[h]
            return m_prev, jnp.maximum(m_prev, jnp.max(st, axis=0, keepdims=True))

        def accumulate(h, st, m_prev, m_cur):
            alpha = jnp.exp2(m_prev - m_cur)
            pt = jnp.exp2(st - m_cur).astype(BF16)
            acc_ref[h] = acc_ref[h] * alpha + _dot(vt_ref[0, h * vrows:(h + 1) * vrows, :], pt)
            m_ref[h] = m_cur

        sts = {h: scores(h) for h in range(min(SCORE_AHEAD + 1, heads))}
        maxes = {0: new_max(0, sts[0])}
        for h in range(heads):
            if h + SCORE_AHEAD + 1 < heads:
                sts[h + SCORE_AHEAD + 1] = scores(h + SCORE_AHEAD + 1)
            if h + 1 < heads:
                maxes[h + 1] = new_max(h + 1, sts[h + 1])
            accumulate(h, sts.pop(h), *maxes.pop(h))

    pl.when(ki < qi)(lambda: step(False))

    @pl.when(ki == qi)
    def _last():
        step(True)
        ot = jnp.concatenate([acc_ref[h, :vhead, :] / acc_ref[h, vhead:vhead + 1, :] for h in range(heads)], axis=0)
        o_ref[...] = ot.T * _silu(gate_ref[...])


def _prompt_attn(q, k, vt, gate, nb, seq, heads, vhead, tile):
    nq = seq // tile
    pairs = [(a, b) for a in range(nq) for b in range(a + 1)]
    qi_tab = jnp.asarray([a for a, _ in pairs], jnp.int32)
    ki_tab = jnp.asarray([b for _, b in pairs], jnp.int32)
    vrows = _vt_rows(vhead)
    grid_spec = pltpu.PrefetchScalarGridSpec(
        num_scalar_prefetch=2,
        grid=(nb, len(pairs)),
        in_specs=[pl.BlockSpec((tile, heads * LANES), lambda b, p, qt, kt: (b * nq + qt[p], 0)),
                  pl.BlockSpec((tile, heads * LANES), lambda b, p, qt, kt: (b * nq + kt[p], 0)),
                  pl.BlockSpec((1, heads * vrows, tile), lambda b, p, qt, kt: (b, 0, kt[p])),
                  pl.BlockSpec((tile, heads * vhead), lambda b, p, qt, kt: (b * nq + qt[p], 0))],
        out_specs=pl.BlockSpec((tile, heads * vhead), lambda b, p, qt, kt: (b * nq + qt[p], 0)),
        scratch_shapes=[pltpu.VMEM((heads, 1, tile), F32),
                        pltpu.VMEM((heads, vrows, tile), F32)])
    return pl.pallas_call(
        functools.partial(_prompt_attn_kernel, heads=heads, vhead=vhead, tile=tile),
        grid_spec=grid_spec,
        out_shape=jax.ShapeDtypeStruct((nb * seq, heads * vhead), F32),
        compiler_params=_cparams("parallel", "arbitrary"),
        name="mla_prompt_attn",
    )(qi_tab, ki_tab, q, k, vt, gate)


def _sample_attn_kernel(pt_ref, qlat_ref, qpe_ref, ckvn_ref, kpen_ref, ckv_hbm, kpe_hbm, o_ref,
                        kbuf, pbuf, sem, m_ref, l_ref, acc_ref, *, layer, npages, heads, tnew, page, nchains):
    b = pl.program_id(0)
    s = pl.program_id(1)
    nb = pl.num_programs(0)
    ns = pl.num_programs(1)
    t = b * ns + s
    slot = lax.rem(t, 2)
    rows = heads * tnew

    def page_copies(bb, ss, sl):
        out = []
        for i in range(npages):
            pid = pt_ref[bb, ss * npages + i]
            out.append(pltpu.make_async_copy(ckv_hbm.at[layer, pid], kbuf.at[sl, pl.ds(i * page, page)], sem.at[0, sl]))
            out.append(pltpu.make_async_copy(kpe_hbm.at[layer, pid], pbuf.at[sl, :, pl.ds(i * page, page)], sem.at[1, sl]))
        return out

    @pl.when(t == 0)
    def _first():
        for c in page_copies(b, s, slot):
            c.start()

    @pl.when(t + 1 < nb * ns)
    def _prefetch():
        wrap = s + 1 == ns
        for c in page_copies(jnp.where(wrap, b + 1, b), jnp.where(wrap, 0, s + 1), 1 - slot):
            c.start()

    q = qlat_ref[...].reshape(rows, qlat_ref.shape[2]).astype(BF16)
    qp = qpe_ref[...].reshape(rows, qpe_ref.shape[2]).astype(BF16)

    @pl.when(s == 0)
    def _new_tokens():
        cn = ckvn_ref[...].astype(BF16)
        sc = _dot_nt(q, cn) + _dot_nt(qp, kpen_ref[...].astype(BF16))
        tq_ = lax.broadcasted_iota(jnp.int32, (rows, tnew), 0) % tnew
        tk_ = lax.broadcasted_iota(jnp.int32, (rows, tnew), 1)
        sc = jnp.where(tk_ <= tq_, sc, NEG_INF)
        m = jnp.max(sc, axis=-1, keepdims=True)
        pr = jnp.exp2(sc - m)
        m_ref[0] = m
        l_ref[0] = jnp.sum(pr, axis=-1, keepdims=True)
        acc_ref[0] = _dot(pr, ckvn_ref[...])
        for c in range(1, nchains):
            m_ref[c] = jnp.full((rows, 1), NEG_INF, F32)
            l_ref[c] = jnp.zeros((rows, 1), F32)
            acc_ref[c] = jnp.zeros(acc_ref.shape[1:], F32)

    for c in page_copies(b, s, slot):
        c.wait()

    nkeys = npages * page // nchains
    cks, scs = [], []
    for c in range(nchains):
        ck = kbuf[slot, pl.ds(c * nkeys, nkeys), :].astype(BF16)
        kp = pbuf[slot, :, pl.ds(c * nkeys, nkeys)].astype(BF16)
        cks.append(ck)
        scs.append(_dot_nt(q, ck) + _dot(qp, kp))
    for c in range(nchains):
        m_prev = m_ref[c]
        m_cur = jnp.maximum(m_prev, jnp.max(scs[c], axis=-1, keepdims=True))
        alpha = jnp.exp2(m_prev - m_cur)
        pr = jnp.exp2(scs[c] - m_cur)
        l_ref[c] = alpha * l_ref[c] + jnp.sum(pr, axis=-1, keepdims=True)
        acc_ref[c] = acc_ref[c] * alpha + _dot(pr.astype(BF16), cks[c])
        m_ref[c] = m_cur

    @pl.when(s == ns - 1)
    def _fin():
        m = m_ref[0]
        for c in range(1, nchains):
            m = jnp.maximum(m, m_ref[c])
        l = jnp.zeros((rows, 1), F32)
        acc = jnp.zeros(acc_ref.shape[1:], F32)
        for c in range(nchains):
            w = jnp.exp2(m_ref[c] - m)
            l = l + l_ref[c] * w
            acc = acc + acc_ref[c] * w
        o_ref[0] = acc / l


def _sample_attn(page_table, qlat, qpe, ckv_new, kpe_new, cache_ckv, cache_kpe_t, layer, npages, nchains):
    heads, ntok, kv_lora = qlat.shape
    rope = qpe.shape[2]
    nb, n_pages_total = page_table.shape
    tnew = ntok // nb
    page = cache_ckv.shape[2]
    steps = n_pages_total // npages
    rows = heads * tnew
    in_specs = [pl.BlockSpec((heads, tnew, kv_lora), lambda b, s, pt: (0, b, 0)),
                pl.BlockSpec((heads, tnew, rope), lambda b, s, pt: (0, b, 0)),
                pl.BlockSpec((tnew, kv_lora), lambda b, s, pt: (b, 0)),
                pl.BlockSpec((tnew, rope), lambda b, s, pt: (b, 0)),
                pl.BlockSpec(memory_space=pl.ANY),
                pl.BlockSpec(memory_space=pl.ANY)]
    grid_spec = pltpu.PrefetchScalarGridSpec(
        num_scalar_prefetch=1,
        grid=(nb, steps),
        in_specs=in_specs,
        out_specs=pl.BlockSpec((1, rows, kv_lora), lambda b, s, pt: (b, 0, 0)),
        scratch_shapes=[pltpu.VMEM((2, npages * page, kv_lora), F32),
                        pltpu.VMEM((2, rope, npages * page), F32),
                        pltpu.SemaphoreType.DMA((2, 2)),
                        pltpu.VMEM((nchains, rows, 1), F32), pltpu.VMEM((nchains, rows, 1), F32),
                        pltpu.VMEM((nchains, rows, kv_lora), F32)])
    return pl.pallas_call(
        functools.partial(_sample_attn_kernel, layer=layer, npages=npages, heads=heads, tnew=tnew, page=page,
                          nchains=nchains),
        grid_spec=grid_spec,
        out_shape=jax.ShapeDtypeStruct((nb, rows, kv_lora), F32),
        compiler_params=_cparams("arbitrary", "arbitrary"),
        name="mla_sample_attn",
    )(page_table, qlat, qpe, ckv_new, kpe_new, cache_ckv, cache_kpe_t)


def _sample_post_kernel(ol_ref, wuv_ref, gate_ref, o_ref, *, heads, tnew, vhead):
    bt = ol_ref.shape[0]
    for h in range(heads):
        x = ol_ref[:, h * tnew:(h + 1) * tnew, :].reshape(bt * tnew, ol_ref.shape[2]).astype(BF16)
        g = gate_ref[:, h * vhead:(h + 1) * vhead]
        o_ref[:, h * vhead:(h + 1) * vhead] = _dot(x, wuv_ref[h]) * _silu(g)


def _sample_post(olat, wuv, gate, tnew, bt):
    nb, rows, kv_lora = olat.shape
    heads, _, vhead = wuv.shape
    ntok = nb * tnew
    return pl.pallas_call(
        functools.partial(_sample_post_kernel, heads=heads, tnew=tnew, vhead=vhead),
        grid=(nb // bt,),
        in_specs=[pl.BlockSpec((bt, rows, kv_lora), lambda i: (i, 0, 0)),
                  _const_spec(wuv.shape),
                  pl.BlockSpec((bt * tnew, heads * vhead), lambda i: (i, 0))],
        out_specs=pl.BlockSpec((bt * tnew, heads * vhead), lambda i: (i, 0)),
        out_shape=jax.ShapeDtypeStruct((ntok, heads * vhead), F32),
        compiler_params=_cparams("parallel"),
        name="mla_sample_post",
    )(olat, wuv, gate)


def _rope_tables(pos, rope, nope):
    half = rope // 2
    inv = 1.0 / (ROPE_BASE ** (jnp.arange(0, rope, 2, dtype=F32) / rope))
    ang = pos[:, None] * inv[None, :]
    cos, sin = jnp.cos(ang), jnp.sin(ang)
    n = pos.shape[0]
    cc = jnp.concatenate([cos, cos], axis=1)
    ss = jnp.concatenate([sin, sin], axis=1)
    cosk = jnp.pad(cc, ((0, 0), (0, LANES - rope)))
    sink = jnp.pad(ss, ((0, 0), (0, LANES - rope)))
    cosq = jnp.concatenate([jnp.ones((n, nope), F32), cc, jnp.zeros((n, LANES - nope - rope), F32)], axis=1)
    sinq = jnp.concatenate([jnp.zeros((n, nope), F32), ss, jnp.zeros((n, LANES - nope - rope), F32)], axis=1)
    del half
    return cosq, sinq, cosk, sink


def _rot_cols(w, half):
    return jnp.concatenate([-w[..., half:], w[..., :half]], axis=-1)


def _mla_weights(w_in, qn, kvn, w_uq, w_uk, w_uv, q_lora, kv_lora, rope, nope):
    d = w_in.shape[0]
    heads = w_uq.shape[1]
    vhead = w_uv.shape[2]
    o1, o2, o3 = q_lora, q_lora + kv_lora, q_lora + kv_lora + rope
    pad = LANES - rope
    wkpe = w_in[:, o2:o3]
    w_in_big = jnp.concatenate([w_in[:, :o2], jnp.pad(wkpe, ((0, 0), (0, pad))),
                                jnp.pad(_rot_cols(wkpe, rope // 2), ((0, 0), (0, pad))), w_in[:, o3:]], axis=1)
    q_nope, q_pe = w_uq[:, :, :nope], w_uq[:, :, nope:]
    zpad = jnp.zeros((q_lora, heads, LANES - nope - rope), F32)
    wq = jnp.concatenate([q_nope, q_pe, zpad], axis=2).reshape(q_lora, heads * LANES)
    wqr = jnp.concatenate([jnp.zeros_like(q_nope), _rot_cols(q_pe, rope // 2), zpad], axis=2).reshape(q_lora, heads * LANES)
    wk_lat = jnp.concatenate([w_uk, jnp.zeros((kv_lora, heads, LANES - nope), F32)], axis=2).reshape(kv_lora, heads * LANES)
    place = jnp.zeros((LANES, heads, LANES), F32).at[jnp.arange(rope), :, nope + jnp.arange(rope)].set(1.0)
    wk = jnp.concatenate([wk_lat, place.reshape(LANES, heads * LANES)], axis=0)
    wvt = w_uv.transpose(1, 2, 0).reshape(heads * vhead, kv_lora)
    return {"in": w_in_big.astype(BF16), "qn": qn.reshape(1, -1), "kvn": kvn.reshape(1, -1),
            "q": wq.astype(BF16), "qrot": wqr.astype(BF16), "k": wk.astype(BF16), "vt": wvt.astype(BF16),
            "ukt": w_uk.transpose(1, 2, 0).astype(BF16), "uv": w_uv.transpose(1, 0, 2).astype(BF16)}


def kernel(x_prompt, x_sample, state_ssm, state_conv, cache_ckv, cache_kpe, page_table, norm_w, final_norm_w,
           ssd_w_in, ssd_conv_w, ssd_conv_b, ssd_dt_bias, ssd_a_log, ssd_d, ssd_norm_w, ssd_w_out,
           mla_w_in, mla_q_norm_w, mla_kv_norm_w, mla_w_uq, mla_w_uk, mla_w_uv, mla_w_out):
    nb, seq, d = x_prompt.shape
    db, dseq, _ = x_sample.shape
    depth = norm_w.shape[0]
    tp, ts = nb * seq, db * dseq
    _, _, heads_s, headdim, state = state_ssm.shape
    d_inner = heads_s * headdim
    conv_dim = ssd_conv_w.shape[2]
    gn = (conv_dim - d_inner) // 2
    groups = gn // state
    q_lora, heads_m, qk = mla_w_uq.shape[1:]
    kv_lora = mla_w_uk.shape[1]
    nope = mla_w_uk.shape[3]
    rope = qk - nope
    vhead = mla_w_uv.shape[3]
    past = page_table.shape[1] * cache_ckv.shape[2]
    scale = (nope + rope) ** -0.5 * LOG2E

    assert seq % SSD_CHUNK == 0 and dseq <= SSD_CHUNK and tp % dseq == 0
    tm = 512
    assert tp % tm == 0 and ts % tm == 0

    h_all = jnp.concatenate([x_prompt.reshape(tp, d), x_sample.reshape(ts, d)], axis=0)

    e = (jnp.arange(d_inner)[None, :] // headdim == jnp.arange(LANES)[:, None]).astype(BF16)
    tab_p = _rope_tables(jnp.arange(seq, dtype=F32), rope, nope)
    tab_s = _rope_tables(jnp.tile(past + jnp.arange(dseq, dtype=F32), ts // dseq), rope, nope)
    mla_dims = (heads_m, q_lora, kv_lora, rope, nope, vhead, scale)

    cache_kpe_t = jnp.swapaxes(cache_kpe, 2, 3)
    n_ssd = (depth + 1) // 2
    tail = SSD_CONV_W - 1

    p_conv, p_ckv, p_kpe = [], [], []
    s_conv, s_ckv, s_kpe = [], [], []
    p_ssm = s_ssm = None
    for i in range(depth):
        j = i // 2
        nw = norm_w[i].reshape(1, d)
        fin_w = final_norm_w.reshape(1, d) if i == depth - 1 else None
        if i % 2 == 0:
            w_big = jnp.pad(ssd_w_in[j], ((0, 0), (0, LANES - heads_s))).astype(BF16)
            z, xbc, dt = _ssd_in_proj(h_all, nw, w_big, d_inner, conv_dim, tm)
            params = (ssd_conv_w[j], ssd_conv_b[j].reshape(1, -1),
                      jnp.pad(ssd_dt_bias[j], (0, LANES - heads_s)).reshape(1, LANES),
                      jnp.pad(ssd_a_log[j], (0, LANES - heads_s)).reshape(1, LANES),
                      jnp.repeat(ssd_d[j], headdim).reshape(1, d_inner),
                      ssd_norm_w[j].reshape(1, d_inner), e, (heads_s, headdim, groups, state))
            y_p, p_ssm = _ssd_scan(xbc, z, dt, 0, nb, seq // SSD_CHUNK, SSD_CHUNK, params, j, n_ssd,
                                   None, None, p_ssm)
            y_s, s_ssm = _ssd_scan(xbc, z, dt, tp // dseq, db, 1, dseq, params, j, n_ssd,
                                   state_conv, state_ssm, s_ssm)
            h_all = _out_proj(y_p, y_s, ssd_w_out[j].astype(BF16), h_all, fin_w, tm)
            p_conv.append(jnp.stack([xbc[(b + 1) * seq - tail:(b + 1) * seq] for b in range(nb)]))
            s_conv.append(xbc[tp:].reshape(db, dseq, conv_dim)[:, dseq - tail:])
        else:
            w = _mla_weights(mla_w_in[j], mla_q_norm_w[j], mla_kv_norm_w[j], mla_w_uq[j], mla_w_uk[j], mla_w_uv[j],
                             q_lora, kv_lora, rope, nope)
            tmp = 256
            ckv_p, kpe_p, gate_p, q_p, k_p, vt_p = _mla_proj(h_all, 0, tp, tmp, seq // tmp, nw, w, tab_p, mla_dims, True)
            ckv_s, kpe_s, gate_s, qlat_s, qpe_s = _mla_proj(h_all, tp // tmp, ts, tmp, ts // tmp, nw, w, tab_s,
                                                            mla_dims, False)
            og_p = _prompt_attn(q_p, k_p, vt_p, gate_p, nb, seq, heads_m, vhead, math.gcd(ATTN_TILE, seq))
            npg = math.gcd(PAGES_PER_STEP, page_table.shape[1])
            olat = _sample_attn(page_table, qlat_s, qpe_s, ckv_s, kpe_s, cache_ckv, cache_kpe_t, j, npg, DECODE_CHAINS)
            og_s = _sample_post(olat, w["uv"], gate_s, dseq, math.gcd(16, db))
            h_all = _out_proj(og_p, og_s, mla_w_out[j].astype(BF16), h_all, fin_w, tm)
            p_ckv.append(ckv_p.reshape(nb, seq, kv_lora))
            p_kpe.append(kpe_p.reshape(nb, seq, rope))
            s_ckv.append(ckv_s.reshape(db, dseq, kv_lora))
            s_kpe.append(kpe_s.reshape(db, dseq, rope))
    y_prompt = h_all[:tp].reshape(nb, seq, d)
    y_sample = h_all[tp:].reshape(db, dseq, d)
    return (y_prompt, y_sample,
            p_ssm, jnp.stack(p_conv), jnp.stack(p_ckv), jnp.stack(p_kpe),
            s_ssm, jnp.stack(s_conv), jnp.stack(s_ckv), jnp.stack(s_kpe))
```

```python
import functools
import math

import jax
import jax.numpy as jnp
from jax import lax
from jax.experimental import pallas as pl
from jax.experimental.pallas import tpu as pltpu

F32 = jnp.float32
BF16 = jnp.bfloat16

NORM_EPS = 1e-6
NEG_INF = -1e30
ROPE_BASE = 10000.0
LOG2E = 1.4426950408889634

LANES = 128
VMEM_LIMIT_BYTES = 56 * 1024 * 1024

SSD_CHUNK = 128
SSD_CONV_W = 4
PAGES_PER_STEP = 64
ATTN_TILE = 512
DECODE_CHAINS = 4
SCORE_AHEAD = 1


def _cparams(*sem):
    return pltpu.CompilerParams(dimension_semantics=sem, vmem_limit_bytes=VMEM_LIMIT_BYTES)


def _const_spec(shape):
    nd = len(shape)
    return pl.BlockSpec(shape, lambda *_: (0,) * nd, pipeline_mode=pl.Buffered(1))


def _silu(x):
    h = 0.5 * x
    return h * jnp.tanh(h) + h


def _dot(a, b):
    return jnp.dot(a, b, preferred_element_type=F32)


def _dot_nt(a, b):
    return lax.dot_general(a, b, (((1,), (1,)), ((), ())), preferred_element_type=F32)


def _rms(x, w):
    return x * lax.rsqrt(jnp.mean(x * x, axis=-1, keepdims=True) + NORM_EPS) * w


def _ssd_in_kernel(x_ref, nw_ref, w_ref, z_ref, xbc_ref, dt_ref, *, d_inner, conv_dim, n_chunk):
    xn = _rms(x_ref[...], nw_ref[...]).astype(BF16)
    col = 0
    for o_ref, width in ((z_ref, d_inner), (xbc_ref, conv_dim), (dt_ref, LANES)):
        for c0 in range(0, width, n_chunk):
            c1 = min(c0 + n_chunk, width)
            o_ref[:, c0:c1] = _dot(xn, w_ref[:, col + c0:col + c1]).astype(o_ref.dtype)
        col += width


def _ssd_in_proj(h, norm_w, w_big, d_inner, conv_dim, tm):
    t, d = h.shape
    n = w_big.shape[1]
    return pl.pallas_call(
        functools.partial(_ssd_in_kernel, d_inner=d_inner, conv_dim=conv_dim, n_chunk=1024),
        grid=(t // tm,),
        in_specs=[pl.BlockSpec((tm, d), lambda i: (i, 0)),
                  _const_spec((1, d)),
                  _const_spec((d, n))],
        out_specs=[pl.BlockSpec((tm, d_inner), lambda i: (i, 0)),
                   pl.BlockSpec((tm, conv_dim), lambda i: (i, 0)),
                   pl.BlockSpec((tm, LANES), lambda i: (i, 0))],
        out_shape=[jax.ShapeDtypeStruct((t, d_inner), F32),
                   jax.ShapeDtypeStruct((t, conv_dim), F32),
                   jax.ShapeDtypeStruct((t, LANES), F32)],
        compiler_params=_cparams("parallel"),
        name="ssd_in_proj",
    )(h, norm_w, w_big)


def _ssd_scan_kernel(*refs, rows, has_init, has_prev, layer, heads, headdim, groups, state):
    if has_prev:
        refs = refs[:-5] + refs[-4:]
    if has_init:
        (xbc_ref, z_ref, dt_ref, cinit_ref, sinit_ref, cw_ref, cb_ref, dtb_ref, alog_ref, dexp_ref,
         nw_ref, e_ref, y_ref, sout_ref, win_ref, st_ref) = refs
    else:
        (xbc_ref, z_ref, dt_ref, cw_ref, cb_ref, dtb_ref, alog_ref, dexp_ref,
         nw_ref, e_ref, y_ref, sout_ref, win_ref, st_ref) = refs
    L = SSD_CHUNK
    d_inner = heads * headdim
    hpg = heads // groups
    gw = hpg * headdim
    gn = groups * state
    c = pl.program_id(1)
    nc = pl.num_programs(1)

    @pl.when(c == 0)
    def _init():
        win_ref[0:8, :] = jnp.zeros((8, win_ref.shape[1]), F32)
        if has_init:
            win_ref[8 - (SSD_CONV_W - 1):8, :] = cinit_ref[0]
            for g in range(groups):
                sg = sinit_ref[0, g * hpg:(g + 1) * hpg].reshape(gw, state)
                st_ref[g] = sg.T
        else:
            st_ref[...] = jnp.zeros(st_ref.shape, F32)

    R = rows

    def pad_time(v):
        if R == L:
            return v
        return jnp.concatenate([v, jnp.zeros((L - R,) + v.shape[1:], v.dtype)], axis=0)

    win_ref[8:8 + R, :] = xbc_ref[...]

    ext = win_ref[0:8 + R, :]
    acc = cw_ref[0:1, :] * ext
    for k in range(1, SSD_CONV_W):
        acc = pltpu.roll(acc, 1, axis=0) + cw_ref[k:k + 1, :] * ext
    conv = acc[8:8 + R] + cb_ref[...]
    if R == L:
        win_ref[0:8, :] = win_ref[L:L + 8, :]
    xc = _silu(conv)

    dt = dt_ref[...] + dtb_ref[...]
    dt = jnp.maximum(dt, 0.0) + jnp.log(1.0 + jnp.exp(-jnp.abs(dt)))
    a = pad_time(dt * (-jnp.exp(alog_ref[...])))

    ri = lax.broadcasted_iota(jnp.int32, (L, L), 0)
    ci = lax.broadcasted_iota(jnp.int32, (L, L), 1)
    causal_full = ri >= ci
    tril = causal_full.astype(F32)
    a_cs_full = jnp.dot(tril, a, precision=lax.Precision.HIGHEST, preferred_element_type=F32)
    a_cst = a_cs_full.T
    a_cs = a_cs_full[:R]
    a_last = a_cs_full[L - 1:L, :]
    causal = causal_full[:R]

    e = e_ref[...]

    def expand(v):
        hi = v.astype(BF16)
        lo = (v - hi.astype(F32)).astype(BF16)
        return _dot(jnp.concatenate([hi, lo], axis=1), e)

    dt_x = expand(dt)
    din_x = expand(jnp.exp(a_cs))
    dend_x = expand(jnp.exp(a_last - a_cs))
    cdec_x = din_x[R - 1:R, :]

    rb = lax.broadcasted_iota(jnp.int32, (hpg * L, gw), 0) // L
    cbk = lax.broadcasted_iota(jnp.int32, (hpg * L, gw), 1) // headdim
    blockdiag = (rb == cbk).astype(BF16)

    def group_matmuls(g):
        bg = pad_time(xc[:, d_inner + g * state:d_inner + (g + 1) * state])
        cg16 = xc[:, d_inner + gn + g * state:d_inner + gn + (g + 1) * state].astype(BF16)
        st = st_ref[g]
        return bg, _dot_nt(cg16, bg.astype(BF16)), st, _dot(cg16, st.astype(BF16))

    ahead = group_matmuls(0)
    for g in range(groups):
        sl = slice(g * gw, (g + 1) * gw)
        bg, gmat, st, y_off = ahead
        if g + 1 < groups:
            ahead = group_matmuls(g + 1)
        xg = xc[:, sl]
        ms = []
        for hh in range(hpg):
            hd = g * hpg + hh
            seg = a_cs[:, hd:hd + 1] - a_cst[hd:hd + 1, :]
            decay = jnp.where(causal, jnp.exp(jnp.where(causal, seg, 0.0)), 0.0)
            ms.append((gmat * decay).astype(BF16))
        mcat = jnp.concatenate(ms, axis=1)
        xdt = xg * dt_x[:, sl]
        xdt16 = pad_time(xdt).astype(BF16)
        bd = jnp.concatenate([xdt16] * hpg, axis=0) * blockdiag
        y = _dot(mcat, bd) + y_off * din_x[:, sl] + dexp_ref[:, sl] * xg
        xd = pad_time(xdt * dend_x[:, sl]).astype(BF16)
        st_ref[g] = st * cdec_x[:, sl] + _dot(bg.T.astype(BF16), xd)
        yz = y * _silu(z_ref[:, sl])
        yn = yz * lax.rsqrt(jnp.mean(yz * yz, axis=-1, keepdims=True) + NORM_EPS) * nw_ref[:, sl]
        y_ref[:, sl] = yn.astype(y_ref.dtype)

    @pl.when(c == nc - 1)
    def _fin():
        if has_prev:
            out = sout_ref.at[0]
        else:
            out = sout_ref.at[layer, 0]
            for other in range(sout_ref.shape[0]):
                if other != layer:
                    sout_ref[other] = jnp.zeros(sout_ref.shape[1:], F32)
        for g in range(groups):
            out[g * hpg:(g + 1) * hpg] = st_ref[g].T.reshape(hpg, headdim, state)


def _ssd_scan(xbc, z, dt, row0, nseq, nchunks, rows, params, layer, n_layers, conv_init, ssm_init, state_out):
    cw, cb, dtb, alog, dexp, nw, e, dims = params
    heads, headdim, groups, state = dims
    d_inner = heads * headdim
    conv_dim = xbc.shape[1]
    has_init = conv_init is not None
    assert rows == SSD_CHUNK or nchunks == 1
    tok = lambda b, c: (row0 + b * nchunks + c, 0)
    out_tok = lambda b, c: (b * nchunks + c, 0)
    in_specs = [pl.BlockSpec((rows, conv_dim), tok),
                pl.BlockSpec((rows, d_inner), tok),
                pl.BlockSpec((rows, LANES), tok)]
    args = [xbc, z, dt]
    if has_init:
        in_specs += [pl.BlockSpec((None, 1) + conv_init.shape[2:], lambda b, c: (layer, b, 0, 0)),
                     pl.BlockSpec((None, 1) + ssm_init.shape[2:], lambda b, c: (layer, b, 0, 0, 0))]
        args += [conv_init, ssm_init]
    consts = [cw, cb, dtb, alog, dexp, nw, e]
    in_specs += [_const_spec(x.shape) for x in consts]
    args += consts
    aliases = {}
    if state_out is not None:
        aliases = {len(args): 1}
        in_specs.append(pl.BlockSpec(memory_space=pl.ANY))
        args.append(state_out)
        state_spec = pl.BlockSpec((None, 1, heads, headdim, state), lambda b, c: (layer, b, 0, 0, 0))
    else:
        state_spec = pl.BlockSpec((n_layers, 1, heads, headdim, state), lambda b, c: (0, b, 0, 0, 0))
    return pl.pallas_call(
        functools.partial(_ssd_scan_kernel, rows=rows, has_init=has_init, has_prev=state_out is not None,
                          layer=layer, heads=heads, headdim=headdim, groups=groups, state=state),
        grid=(nseq, nchunks),
        in_specs=in_specs,
        out_specs=[pl.BlockSpec((rows, d_inner), out_tok), state_spec],
        out_shape=[jax.ShapeDtypeStruct((nseq * nchunks * rows, d_inner), F32),
                   jax.ShapeDtypeStruct((n_layers, nseq, heads, headdim, state), F32)],
        scratch_shapes=[pltpu.VMEM((SSD_CHUNK + 8 + 8, conv_dim), F32),
                        pltpu.VMEM((groups, state, (heads // groups) * headdim), F32)],
        input_output_aliases=aliases,
        compiler_params=_cparams("parallel", "arbitrary"),
        name="ssd_scan_init" if has_init else "ssd_scan",
    )(*args)


def _out_proj_kernel(*refs, final, n_first):
    if final:
        a0_ref, a1_ref, w_ref, res_ref, fw_ref, o0_ref, o1_ref = refs
    else:
        a0_ref, a1_ref, w_ref, res_ref, o0_ref = refs
        o1_ref = o0_ref

    def run(a_ref, o_ref):
        hnew = res_ref[...] + _dot(a_ref[...].astype(BF16), w_ref[...])
        if final:
            hnew = _rms(hnew, fw_ref[...])
        o_ref[...] = hnew

    i = pl.program_id(0)
    pl.when(i < n_first)(lambda: run(a0_ref, o0_ref))
    pl.when(i >= n_first)(lambda: run(a1_ref, o1_ref))


def _out_proj(a0, a1, w, res, final_w, tm):
    k = a0.shape[1]
    t = a0.shape[0] + a1.shape[0]
    n_first = a0.shape[0] // tm
    d = w.shape[1]
    final = final_w is not None
    first = lambda i: (jnp.minimum(i, n_first - 1), 0)
    second = lambda i: (jnp.maximum(i - n_first, 0), 0)
    in_specs = [pl.BlockSpec((tm, k), first), pl.BlockSpec((tm, k), second),
                _const_spec((k, d)), pl.BlockSpec((tm, d), lambda i: (i, 0))]
    args = [a0, a1, w, res]
    if final:
        in_specs.append(_const_spec((1, d)))
        args.append(final_w)
        out_specs = [pl.BlockSpec((tm, d), first), pl.BlockSpec((tm, d), second)]
        out_shape = [jax.ShapeDtypeStruct((a0.shape[0], d), F32), jax.ShapeDtypeStruct((a1.shape[0], d), F32)]
    else:
        out_specs = pl.BlockSpec((tm, d), lambda i: (i, 0))
        out_shape = jax.ShapeDtypeStruct((t, d), F32)
    return pl.pallas_call(
        functools.partial(_out_proj_kernel, final=final, n_first=n_first),
        grid=(t // tm,),
        in_specs=in_specs,
        out_specs=out_specs,
        out_shape=out_shape,
        compiler_params=_cparams("arbitrary" if final else "parallel"),
        name="out_proj_final" if final else "out_proj",
    )(*args)


def _mla_proj_kernel(*refs, prompt, heads, q_lora, kv_lora, rope, nope, scale):
    if prompt:
        (x_ref, nw_ref, win_ref, qnw_ref, kvnw_ref, wq_ref, wqr_ref, cosq_ref, sinq_ref, cosk_ref, sink_ref,
         wk_ref, wvt_ref, ckv_ref, kpe_ref, gate_ref, q_ref, k_ref, vt_ref) = refs
    else:
        (x_ref, nw_ref, win_ref, qnw_ref, kvnw_ref, wq_ref, wqr_ref, cosq_ref, sinq_ref, cosk_ref, sink_ref,
         wukt_ref, ckv_ref, kpe_ref, gate_ref, qlat_ref, qpe_ref) = refs
    xn = _rms(x_ref[...], nw_ref[...]).astype(BF16)
    o1 = q_lora
    o2 = o1 + kv_lora
    cq = _rms(_dot(xn, win_ref[:, 0:o1]), qnw_ref[...]).astype(BF16)
    ckv = _rms(_dot(xn, win_ref[:, o1:o2]), kvnw_ref[...])
    ckv_ref[...] = ckv
    kp = _dot(xn, win_ref[:, o2:o2 + 2 * LANES])
    kpe = kp[:, :LANES] * cosk_ref[...] + kp[:, LANES:] * sink_ref[...]
    kpe_ref[...] = kpe[:, :rope]
    gate_ref[...] = _dot(xn, win_ref[:, o2 + 2 * LANES:])
    hc = 8
    cosq = jnp.concatenate([cosq_ref[...] * scale] * hc, axis=1)
    sinq = jnp.concatenate([sinq_ref[...] * scale] * hc, axis=1)
    for h0 in range(0, heads, hc):
        cs = slice(h0 * LANES, (h0 + hc) * LANES)
        qc = _dot(cq, wq_ref[:, cs]) * cosq + _dot(cq, wqr_ref[:, cs]) * sinq
        if prompt:
            q_ref[:, cs] = qc.astype(BF16)
        else:
            for h in range(h0, h0 + hc):
                off = (h - h0) * LANES
                qlat_ref[h] = _dot(qc[:, off:off + nope].astype(BF16), wukt_ref[h])
                qpe_ref[h] = qc[:, off + nope:off + nope + rope]
    if prompt:
        ckv16 = ckv.astype(BF16)
        kin = jnp.concatenate([ckv16, kpe.astype(BF16)], axis=1)
        for c0 in range(0, heads * LANES, 1024):
            k_ref[:, c0:c0 + 1024] = _dot(kin, wk_ref[:, c0:c0 + 1024]).astype(BF16)
        vt = _dot_nt(wvt_ref[...], ckv16)
        vhead = vt.shape[0] // heads
        npad = vt_ref.shape[1] // heads - vhead
        pad = (lax.broadcasted_iota(jnp.int32, (npad, vt.shape[1]), 0) == 0).astype(F32)
        pieces = []
        for h in range(heads):
            pieces += [vt[h * vhead:(h + 1) * vhead], pad]
        vt_ref[0] = jnp.concatenate(pieces, axis=0).astype(BF16)


def _mla_proj(h_all, row0_blocks, ntok, tm, seq_blocks, nw, w, tables, dims, prompt):
    heads, q_lora, kv_lora, rope, nope, vhead, scale = dims
    d = h_all.shape[1]
    cosq, sinq, cosk, sink = tables
    tok = lambda i: (row0_blocks + i, 0)
    out_tok = lambda i: (i, 0)
    pos = lambda i: (i % seq_blocks, 0)
    shared = [w["in"], w["qn"], w["kvn"], w["q"], w["qrot"]]
    in_specs = [pl.BlockSpec((tm, d), tok), _const_spec(nw.shape)] + [_const_spec(x.shape) for x in shared]
    in_specs += [pl.BlockSpec((tm, LANES), pos)] * 4
    args = [h_all, nw] + shared + [cosq, sinq, cosk, sink]
    gw = heads * vhead
    out_specs = [pl.BlockSpec((tm, kv_lora), out_tok), pl.BlockSpec((tm, rope), out_tok), pl.BlockSpec((tm, gw), out_tok)]
    out_shape = [jax.ShapeDtypeStruct((ntok, kv_lora), F32), jax.ShapeDtypeStruct((ntok, rope), F32),
                 jax.ShapeDtypeStruct((ntok, gw), F32)]
    if prompt:
        nb = ntok // (seq_blocks * tm)
        in_specs += [_const_spec(w["k"].shape), _const_spec(w["vt"].shape)]
        args += [w["k"], w["vt"]]
        vr = heads * _vt_rows(vhead)
        out_specs += [pl.BlockSpec((tm, heads * LANES), out_tok), pl.BlockSpec((tm, heads * LANES), out_tok),
                      pl.BlockSpec((1, vr, tm), lambda i: (i // seq_blocks, 0, i % seq_blocks))]
        out_shape += [jax.ShapeDtypeStruct((ntok, heads * LANES), BF16), jax.ShapeDtypeStruct((ntok, heads * LANES), BF16),
                      jax.ShapeDtypeStruct((nb, vr, seq_blocks * tm), BF16)]
    else:
        in_specs += [_const_spec(w["ukt"].shape)]
        args += [w["ukt"]]
        out_specs += [pl.BlockSpec((heads, tm, kv_lora), lambda i: (0, i, 0)),
                      pl.BlockSpec((heads, tm, rope), lambda i: (0, i, 0))]
        out_shape += [jax.ShapeDtypeStruct((heads, ntok, kv_lora), F32), jax.ShapeDtypeStruct((heads, ntok, rope), F32)]
    return pl.pallas_call(
        functools.partial(_mla_proj_kernel, prompt=prompt, heads=heads, q_lora=q_lora, kv_lora=kv_lora, rope=rope,
                          nope=nope, scale=scale),
        grid=(ntok // tm,),
        in_specs=in_specs,
        out_specs=out_specs,
        out_shape=out_shape,
        compiler_params=_cparams("parallel"),
        name="mla_proj_prompt" if prompt else "mla_proj_sample",
    )(*args)


def _vt_rows(vhead):
    bf16_sublanes = 16
    return -(-(vhead + 1) // bf16_sublanes) * bf16_sublanes


def _prompt_attn_kernel(qi_ref, ki_ref, q_ref, k_ref, vt_ref, gate_ref, o_ref, m_ref, acc_ref, *,
                        heads, vhead, tile):
    p = pl.program_id(1)
    qi = qi_ref[p]
    ki = ki_ref[p]
    vrows = _vt_rows(vhead)

    @pl.when(ki == 0)
    def _init():
        m_ref[...] = jnp.full(m_ref.shape, NEG_INF, F32)
        acc_ref[...] = jnp.zeros(acc_ref.shape, F32)

    def step(diagonal):
        if diagonal:
            visible = (lax.broadcasted_iota(jnp.int32, (tile, tile), 0)
                       <= lax.broadcasted_iota(jnp.int32, (tile, tile), 1))

        def scores(h):
            sl = slice(h * LANES, (h + 1) * LANES)
            st = _dot_nt(k_ref[:, sl], q_ref[:, sl])
            return jnp.where(visible, st, NEG_INF) if diagonal else st

        def new_max(h, st):
            m_prev = m_ref[h]
            return m_prev, jnp.maximum(m_prev, jnp.max(st, axis=0, keepdims=True))

        def accumulate(h, st, m_prev, m_cur):
            alpha = jnp.exp2(m_prev - m_cur)
            pt = jnp.exp2(st - m_cur).astype(BF16)
            acc_ref[h] = acc_ref[h] * alpha + _dot(vt_ref[0, h * vrows:(h + 1) * vrows, :], pt)
            m_ref[h] = m_cur

        sts = {h: scores(h) for h in range(min(SCORE_AHEAD + 1, heads))}
        maxes = {0: new_max(0, sts[0])}
        for h in range(heads):
            if h + SCORE_AHEAD + 1 < heads:
                sts[h + SCORE_AHEAD + 1] = scores(h + SCORE_AHEAD + 1)
            if h + 1 < heads:
                maxes[h + 1] = new_max(h + 1, sts[h + 1])
            accumulate(h, sts.pop(h), *maxes.pop(h))

    pl.when(ki < qi)(lambda: step(False))

    @pl.when(ki == qi)
    def _last():
        step(True)
        ot = jnp.concatenate([acc_ref[h, :vhead, :] / acc_ref[h, vhead:vhead + 1, :] for h in range(heads)], axis=0)
        o_ref[...] = ot.T * _silu(gate_ref[...])


def _prompt_attn(q, k, vt, gate, nb, seq, heads, vhead, tile):
    nq = seq // tile
    pairs = [(a, b) for a in range(nq) for b in range(a + 1)]
    qi_tab = jnp.asarray([a for a, _ in pairs], jnp.int32)
    ki_tab = jnp.asarray([b for _, b in pairs], jnp.int32)
    vrows = _vt_rows(vhead)
    grid_spec = pltpu.PrefetchScalarGridSpec(
        num_scalar_prefetch=2,
        grid=(nb, len(pairs)),
        in_specs=[pl.BlockSpec((tile, heads * LANES), lambda b, p, qt, kt: (b * nq + qt[p], 0)),
                  pl.BlockSpec((tile, heads * LANES), lambda b, p, qt, kt: (b * nq + kt[p], 0)),
                  pl.BlockSpec((1, heads * vrows, tile), lambda b, p, qt, kt: (b, 0, kt[p])),
                  pl.BlockSpec((tile, heads * vhead), lambda b, p, qt, kt: (b * nq + qt[p], 0))],
        out_specs=pl.BlockSpec((tile, heads * vhead), lambda b, p, qt, kt: (b * nq + qt[p], 0)),
        scratch_shapes=[pltpu.VMEM((heads, 1, tile), F32),
                        pltpu.VMEM((heads, vrows, tile), F32)])
    return pl.pallas_call(
        functools.partial(_prompt_attn_kernel, heads=heads, vhead=vhead, tile=tile),
        grid_spec=grid_spec,
        out_shape=jax.ShapeDtypeStruct((nb * seq, heads * vhead), F32),
        compiler_params=_cparams("parallel", "arbitrary"),
        name="mla_prompt_attn",
    )(qi_tab, ki_tab, q, k, vt, gate)


def _sample_attn_kernel(pt_ref, qlat_ref, qpe_ref, ckvn_ref, kpen_ref, ckv_hbm, kpe_hbm, o_ref,
                        kbuf, pbuf, sem, m_ref, l_ref, acc_ref, *, layer, npages, heads, tnew, page, nchains):
    b = pl.program_id(0)
    s = pl.program_id(1)
    nb = pl.num_programs(0)
    ns = pl.num_programs(1)
    t = b * ns + s
    slot = lax.rem(t, 2)
    rows = heads * tnew

    def page_copies(bb, ss, sl):
        out = []
        for i in range(npages):
            pid = pt_ref[bb, ss * npages + i]
            out.append(pltpu.make_async_copy(ckv_hbm.at[layer, pid], kbuf.at[sl, pl.ds(i * page, page)], sem.at[0, sl]))
            out.append(pltpu.make_async_copy(kpe_hbm.at[layer, pid], pbuf.at[sl, :, pl.ds(i * page, page)], sem.at[1, sl]))
        return out

    @pl.when(t == 0)
    def _first():
        for c in page_copies(b, s, slot):
            c.start()

    @pl.when(t + 1 < nb * ns)
    def _prefetch():
        wrap = s + 1 == ns
        for c in page_copies(jnp.where(wrap, b + 1, b), jnp.where(wrap, 0, s + 1), 1 - slot):
            c.start()

    q = qlat_ref[...].reshape(rows, qlat_ref.shape[2]).astype(BF16)
    qp = qpe_ref[...].reshape(rows, qpe_ref.shape[2]).astype(BF16)

    @pl.when(s == 0)
    def _new_tokens():
        cn = ckvn_ref[...].astype(BF16)
        sc = _dot_nt(q, cn) + _dot_nt(qp, kpen_ref[...].astype(BF16))
        tq_ = lax.broadcasted_iota(jnp.int32, (rows, tnew), 0) % tnew
        tk_ = lax.broadcasted_iota(jnp.int32, (rows, tnew), 1)
        sc = jnp.where(tk_ <= tq_, sc, NEG_INF)
        m = jnp.max(sc, axis=-1, keepdims=True)
        pr = jnp.exp2(sc - m)
        m_ref[0] = m
        l_ref[0] = jnp.sum(pr, axis=-1, keepdims=True)
        acc_ref[0] = _dot(pr, ckvn_ref[...])
        for c in range(1, nchains):
            m_ref[c] = jnp.full((rows, 1), NEG_INF, F32)
            l_ref[c] = jnp.zeros((rows, 1), F32)
            acc_ref[c] = jnp.zeros(acc_ref.shape[1:], F32)

    for c in page_copies(b, s, slot):
        c.wait()

    nkeys = npages * page // nchains
    cks, scs = [], []
    for c in range(nchains):
        ck = kbuf[slot, pl.ds(c * nkeys, nkeys), :].astype(BF16)
        kp = pbuf[slot, :, pl.ds(c * nkeys, nkeys)].astype(BF16)
        cks.append(ck)
        scs.append(_dot_nt(q, ck) + _dot(qp, kp))
    for c in range(nchains):
        m_prev = m_ref[c]
        m_cur = jnp.maximum(m_prev, jnp.max(scs[c], axis=-1, keepdims=True))
        alpha = jnp.exp2(m_prev - m_cur)
        pr = jnp.exp2(scs[c] - m_cur)
        l_ref[c] = alpha * l_ref[c] + jnp.sum(pr, axis=-1, keepdims=True)
        acc_ref[c] = acc_ref[c] * alpha + _dot(pr.astype(BF16), cks[c])
        m_ref[c] = m_cur

    @pl.when(s == ns - 1)
    def _fin():
        m = m_ref[0]
        for c in range(1, nchains):
            m = jnp.maximum(m, m_ref[c])
        l = jnp.zeros((rows, 1), F32)
        acc = jnp.zeros(acc_ref.shape[1:], F32)
        for c in range(nchains):
            w = jnp.exp2(m_ref[c] - m)
            l = l + l_ref[c] * w
            acc = acc + acc_ref[c] * w
        o_ref[0] = acc / l


def _sample_attn(page_table, qlat, qpe, ckv_new, kpe_new, cache_ckv, cache_kpe_t, layer, npages, nchains):
    heads, ntok, kv_lora = qlat.shape
    rope = qpe.shape[2]
    nb, n_pages_total = page_table.shape
    tnew = ntok // nb
    page = cache_ckv.shape[2]
    steps = n_pages_total // npages
    rows = heads * tnew
    in_specs = [pl.BlockSpec((heads, tnew, kv_lora), lambda b, s, pt: (0, b, 0)),
                pl.BlockSpec((heads, tnew, rope), lambda b, s, pt: (0, b, 0)),
                pl.BlockSpec((tnew, kv_lora), lambda b, s, pt: (b, 0)),
                pl.BlockSpec((tnew, rope), lambda b, s, pt: (b, 0)),
                pl.BlockSpec(memory_space=pl.ANY),
                pl.BlockSpec(memory_space=pl.ANY)]
    grid_spec = pltpu.PrefetchScalarGridSpec(
        num_scalar_prefetch=1,
        grid=(nb, steps),
        in_specs=in_specs,
        out_specs=pl.BlockSpec((1, rows, kv_lora), lambda b, s, pt: (b, 0, 0)),
        scratch_shapes=[pltpu.VMEM((2, npages * page, kv_lora), F32),
                        pltpu.VMEM((2, rope, npages * page), F32),
                        pltpu.SemaphoreType.DMA((2, 2)),
                        pltpu.VMEM((nchains, rows, 1), F32), pltpu.VMEM((nchains, rows, 1), F32),
                        pltpu.VMEM((nchains, rows, kv_lora), F32)])
    return pl.pallas_call(
        functools.partial(_sample_attn_kernel, layer=layer, npages=npages, heads=heads, tnew=tnew, page=page,
                          nchains=nchains),
        grid_spec=grid_spec,
        out_shape=jax.ShapeDtypeStruct((nb, rows, kv_lora), F32),
        compiler_params=_cparams("arbitrary", "arbitrary"),
        name="mla_sample_attn",
    )(page_table, qlat, qpe, ckv_new, kpe_new, cache_ckv, cache_kpe_t)


def _sample_post_kernel(ol_ref, wuv_ref, gate_ref, o_ref, *, heads, tnew, vhead):
    bt = ol_ref.shape[0]
    for h in range(heads):
        x = ol_ref[:, h * tnew:(h + 1) * tnew, :].reshape(bt * tnew, ol_ref.shape[2]).astype(BF16)
        g = gate_ref[:, h * vhead:(h + 1) * vhead]
        o_ref[:, h * vhead:(h + 1) * vhead] = _dot(x, wuv_ref[h]) * _silu(g)


def _sample_post(olat, wuv, gate, tnew, bt):
    nb, rows, kv_lora = olat.shape
    heads, _, vhead = wuv.shape
    ntok = nb * tnew
    return pl.pallas_call(
        functools.partial(_sample_post_kernel, heads=heads, tnew=tnew, vhead=vhead),
        grid=(nb // bt,),
        in_specs=[pl.BlockSpec((bt, rows, kv_lora), lambda i: (i, 0, 0)),
                  _const_spec(wuv.shape),
                  pl.BlockSpec((bt * tnew, heads * vhead), lambda i: (i, 0))],
        out_specs=pl.BlockSpec((bt * tnew, heads * vhead), lambda i: (i, 0)),
        out_shape=jax.ShapeDtypeStruct((ntok, heads * vhead), F32),
        compiler_params=_cparams("parallel"),
        name="mla_sample_post",
    )(olat, wuv, gate)


def _rope_tables(pos, rope, nope):
    half = rope // 2
    inv = 1.0 / (ROPE_BASE ** (jnp.arange(0, rope, 2, dtype=F32) / rope))
    ang = pos[:, None] * inv[None, :]
    cos, sin = jnp.cos(ang), jnp.sin(ang)
    n = pos.shape[0]
    cc = jnp.concatenate([cos, cos], axis=1)
    ss = jnp.concatenate([sin, sin], axis=1)
    cosk = jnp.pad(cc, ((0, 0), (0, LANES - rope)))
    sink = jnp.pad(ss, ((0, 0), (0, LANES - rope)))
    cosq = jnp.concatenate([jnp.ones((n, nope), F32), cc, jnp.zeros((n, LANES - nope - rope), F32)], axis=1)
    sinq = jnp.concatenate([jnp.zeros((n, nope), F32), ss, jnp.zeros((n, LANES - nope - rope), F32)], axis=1)
    del half
    return cosq, sinq, cosk, sink


def _rot_cols(w, half):
    return jnp.concatenate([-w[..., half:], w[..., :half]], axis=-1)


def _mla_weights(w_in, qn, kvn, w_uq, w_uk, w_uv, q_lora, kv_lora, rope, nope):
    d = w_in.shape[0]
    heads = w_uq.shape[1]
    vhead = w_uv.shape[2]
    o1, o2, o3 = q_lora, q_lora + kv_lora, q_lora + kv_lora + rope
    pad = LANES - rope
    wkpe = w_in[:, o2:o3]
    w_in_big = jnp.concatenate([w_in[:, :o2], jnp.pad(wkpe, ((0, 0), (0, pad))),
                                jnp.pad(_rot_cols(wkpe, rope // 2), ((0, 0), (0, pad))), w_in[:, o3:]], axis=1)
    q_nope, q_pe = w_uq[:, :, :nope], w_uq[:, :, nope:]
    zpad = jnp.zeros((q_lora, heads, LANES - nope - rope), F32)
    wq = jnp.concatenate([q_nope, q_pe, zpad], axis=2).reshape(q_lora, heads * LANES)
    wqr = jnp.concatenate([jnp.zeros_like(q_nope), _rot_cols(q_pe, rope // 2), zpad], axis=2).reshape(q_lora, heads * LANES)
    wk_lat = jnp.concatenate([w_uk, jnp.zeros((kv_lora, heads, LANES - nope), F32)], axis=2).reshape(kv_lora, heads * LANES)
    place = jnp.zeros((LANES, heads, LANES), F32).at[jnp.arange(rope), :, nope + jnp.arange(rope)].set(1.0)
    wk = jnp.concatenate([wk_lat, place.reshape(LANES, heads * LANES)], axis=0)
    wvt = w_uv.transpose(1, 2, 0).reshape(heads * vhead, kv_lora)
    return {"in": w_in_big.astype(BF16), "qn": qn.reshape(1, -1), "kvn": kvn.reshape(1, -1),
            "q": wq.astype(BF16), "qrot": wqr.astype(BF16), "k": wk.astype(BF16), "vt": wvt.astype(BF16),
            "ukt": w_uk.transpose(1, 2, 0).astype(BF16), "uv": w_uv.transpose(1, 0, 2).astype(BF16)}


def kernel(x_prompt, x_sample, state_ssm, state_conv, cache_ckv, cache_kpe, page_table, norm_w, final_norm_w,
           ssd_w_in, ssd_conv_w, ssd_conv_b, ssd_dt_bias, ssd_a_log, ssd_d, ssd_norm_w, ssd_w_out,
           mla_w_in, mla_q_norm_w, mla_kv_norm_w, mla_w_uq, mla_w_uk, mla_w_uv, mla_w_out):
    nb, seq, d = x_prompt.shape
    db, dseq, _ = x_sample.shape
    depth = norm_w.shape[0]
    tp, ts = nb * seq, db * dseq
    _, _, heads_s, headdim, state = state_ssm.shape
    d_inner = heads_s * headdim
    conv_dim = ssd_conv_w.shape[2]
    gn = (conv_dim - d_inner) // 2
    groups = gn // state
    q_lora, heads_m, qk = mla_w_uq.shape[1:]
    kv_lora = mla_w_uk.shape[1]
    nope = mla_w_uk.shape[3]
    rope = qk - nope
    vhead = mla_w_uv.shape[3]
    past = page_table.shape[1] * cache_ckv.shape[2]
    scale = (nope + rope) ** -0.5 * LOG2E

    assert seq % SSD_CHUNK == 0 and dseq <= SSD_CHUNK and tp % dseq == 0
    tm = 512
    assert tp % tm == 0 and ts % tm == 0

    h_all = jnp.concatenate([x_prompt.reshape(tp, d), x_sample.reshape(ts, d)], axis=0)

    e = (jnp.arange(d_inner)[None, :] // headdim == jnp.arange(LANES)[:, None]).astype(BF16)
    e = jnp.concatenate([e, e], axis=0)
    tab_p = _rope_tables(jnp.arange(seq, dtype=F32), rope, nope)
    tab_s = _rope_tables(jnp.tile(past + jnp.arange(dseq, dtype=F32), ts // dseq), rope, nope)
    mla_dims = (heads_m, q_lora, kv_lora, rope, nope, vhead, scale)

    cache_kpe_t = jnp.swapaxes(cache_kpe, 2, 3)
    n_ssd = (depth + 1) // 2
    tail = SSD_CONV_W - 1

    p_conv, p_ckv, p_kpe = [], [], []
    s_conv, s_ckv, s_kpe = [], [], []
    p_ssm = s_ssm = None
    for i in range(depth):
        j = i // 2
        nw = norm_w[i].reshape(1, d)
        fin_w = final_norm_w.reshape(1, d) if i == depth - 1 else None
        if i % 2 == 0:
            w_big = jnp.pad(ssd_w_in[j], ((0, 0), (0, LANES - heads_s))).astype(BF16)
            z, xbc, dt = _ssd_in_proj(h_all, nw, w_big, d_inner, conv_dim, tm)
            params = (ssd_conv_w[j], ssd_conv_b[j].reshape(1, -1),
                      jnp.pad(ssd_dt_bias[j], (0, LANES - heads_s)).reshape(1, LANES),
                      jnp.pad(ssd_a_log[j], (0, LANES - heads_s)).reshape(1, LANES),
                      jnp.repeat(ssd_d[j], headdim).reshape(1, d_inner),
                      ssd_norm_w[j].reshape(1, d_inner), e, (heads_s, headdim, groups, state))
            y_p, p_ssm = _ssd_scan(xbc, z, dt, 0, nb, seq // SSD_CHUNK, SSD_CHUNK, params, j, n_ssd,
                                   None, None, p_ssm)
            y_s, s_ssm = _ssd_scan(xbc, z, dt, tp // dseq, db, 1, dseq, params, j, n_ssd,
                                   state_conv, state_ssm, s_ssm)
            h_all = _out_proj(y_p, y_s, ssd_w_out[j].astype(BF16), h_all, fin_w, tm)
            p_conv.append(jnp.stack([xbc[(b + 1) * seq - tail:(b + 1) * seq] for b in range(nb)]))
            s_conv.append(xbc[tp:].reshape(db, dseq, conv_dim)[:, dseq - tail:])
        else:
            w = _mla_weights(mla_w_in[j], mla_q_norm_w[j], mla_kv_norm_w[j], mla_w_uq[j], mla_w_uk[j], mla_w_uv[j],
                             q_lora, kv_lora, rope, nope)
            tmp = 256
            ckv_p, kpe_p, gate_p, q_p, k_p, vt_p = _mla_proj(h_all, 0, tp, tmp, seq // tmp, nw, w, tab_p, mla_dims, True)
            ckv_s, kpe_s, gate_s, qlat_s, qpe_s = _mla_proj(h_all, tp // tmp, ts, tmp, ts // tmp, nw, w, tab_s,
                                                            mla_dims, False)
            og_p = _prompt_attn(q_p, k_p, vt_p, gate_p, nb, seq, heads_m, vhead, math.gcd(ATTN_TILE, seq))
            npg = math.gcd(PAGES_PER_STEP, page_table.shape[1])
            olat = _sample_attn(page_table, qlat_s, qpe_s, ckv_s, kpe_s, cache_ckv, cache_kpe_t, j, npg, DECODE_CHAINS)
            og_s = _sample_post(olat, w["uv"], gate_s, dseq, math.gcd(16, db))
            h_all = _out_proj(og_p, og_s, mla_w_out[j].astype(BF16), h_all, fin_w, tm)
            p_ckv.append(ckv_p.reshape(nb, seq, kv_lora))
            p_kpe.append(kpe_p.reshape(nb, seq, rope))
            s_ckv.append(ckv_s.reshape(db, dseq, kv_lora))
            s_kpe.append(kpe_s.reshape(db, dseq, rope))
    y_prompt = h_all[0].reshape(nb, seq, d)
    y_sample = h_all[1].reshape(db, dseq, d)
    return (y_prompt, y_sample,
            p_ssm, jnp.stack(p_conv), jnp.stack(p_ckv), jnp.stack(p_kpe),
            s_ssm, jnp.stack(s_conv), jnp.stack(s_ckv), jnp.stack(s_kpe))
```

```python
import functools
import math

import jax
import jax.numpy as jnp
from jax import lax
from jax.experimental import pallas as pl
from jax.experimental.pallas import tpu as pltpu

F32 = jnp.float32
BF16 = jnp.bfloat16

NORM_EPS = 1e-6
NEG_INF = -1e30
ROPE_BASE = 10000.0
LOG2E = 1.4426950408889634

LANES = 128
VMEM_LIMIT_BYTES = 56 * 1024 * 1024

SSD_CHUNK = 128
SSD_CONV_W = 4
PAGES_PER_STEP = 64
ATTN_TILE = 512
DECODE_CHAINS = 4
SCORE_AHEAD = 1


def _cparams(*sem):
    return pltpu.CompilerParams(dimension_semantics=sem, vmem_limit_bytes=VMEM_LIMIT_BYTES)


def _const_spec(shape):
    nd = len(shape)
    return pl.BlockSpec(shape, lambda *_: (0,) * nd, pipeline_mode=pl.Buffered(1))


def _silu(x):
    h = 0.5 * x
    return h * jnp.tanh(h) + h


def _dot(a, b):
    return jnp.dot(a, b, preferred_element_type=F32)


def _dot_nt(a, b):
    return lax.dot_general(a, b, (((1,), (1,)), ((), ())), preferred_element_type=F32)


def _rms(x, w):
    return x * lax.rsqrt(jnp.mean(x * x, axis=-1, keepdims=True) + NORM_EPS) * w


def _row_specs(h, tm):
    if not isinstance(h, tuple):
        return [pl.BlockSpec((tm, h.shape[1]), lambda i: (i, 0))], [h], None
    n_first = h[0].shape[0] // tm
    width = h[0].shape[1]
    return ([pl.BlockSpec((tm, width), lambda i: (jnp.minimum(i, n_first - 1), 0)),
             pl.BlockSpec((tm, width), lambda i: (jnp.maximum(i - n_first, 0), 0))], list(h), n_first)


def _read_rows(refs, n_first):
    if n_first is None:
        return refs[0][...]
    return jnp.where(pl.program_id(0) < n_first, refs[0][...], refs[1][...])


def _ssd_in_kernel(*refs, n_first, d_inner, conv_dim, n_chunk):
    nx = 1 if n_first is None else 2
    nw_ref, w_ref, z_ref, xbc_ref, dt_ref = refs[nx:]
    xn = _rms(_read_rows(refs[:nx], n_first), nw_ref[...]).astype(BF16)
    col = 0
    for o_ref, width in ((z_ref, d_inner), (xbc_ref, conv_dim), (dt_ref, LANES)):
        for c0 in range(0, width, n_chunk):
            c1 = min(c0 + n_chunk, width)
            o_ref[:, c0:c1] = _dot(xn, w_ref[:, col + c0:col + c1]).astype(o_ref.dtype)
        col += width


def _ssd_in_proj(h, norm_w, w_big, d_inner, conv_dim, tm):
    x_specs, x_args, n_first = _row_specs(h, tm)
    t = sum(x.shape[0] for x in x_args)
    d, n = w_big.shape
    return pl.pallas_call(
        functools.partial(_ssd_in_kernel, n_first=n_first, d_inner=d_inner, conv_dim=conv_dim, n_chunk=1024),
        grid=(t // tm,),
        in_specs=x_specs + [_const_spec((1, d)), _const_spec((d, n))],
        out_specs=[pl.BlockSpec((tm, d_inner), lambda i: (i, 0)),
                   pl.BlockSpec((tm, conv_dim), lambda i: (i, 0)),
                   pl.BlockSpec((tm, LANES), lambda i: (i, 0))],
        out_shape=[jax.ShapeDtypeStruct((t, d_inner), F32),
                   jax.ShapeDtypeStruct((t, conv_dim), F32),
                   jax.ShapeDtypeStruct((t, LANES), F32)],
        compiler_params=_cparams("parallel"),
        name="ssd_in_proj",
    )(*x_args, norm_w, w_big)


def _ssd_scan_kernel(*refs, rows, has_init, has_prev, layer, heads, headdim, groups, state):
    if has_prev:
        refs = refs[:-5] + refs[-4:]
    if has_init:
        (xbc_ref, z_ref, dt_ref, cinit_ref, sinit_ref, cw_ref, cb_ref, dtb_ref, alog_ref, dexp_ref,
         nw_ref, e_ref, y_ref, sout_ref, win_ref, st_ref) = refs
    else:
        (xbc_ref, z_ref, dt_ref, cw_ref, cb_ref, dtb_ref, alog_ref, dexp_ref,
         nw_ref, e_ref, y_ref, sout_ref, win_ref, st_ref) = refs
    L = SSD_CHUNK
    d_inner = heads * headdim
    hpg = heads // groups
    gw = hpg * headdim
    gn = groups * state
    c = pl.program_id(1)
    nc = pl.num_programs(1)

    @pl.when(c == 0)
    def _init():
        win_ref[0:8, :] = jnp.zeros((8, win_ref.shape[1]), F32)
        if has_init:
            win_ref[8 - (SSD_CONV_W - 1):8, :] = cinit_ref[0]
            for g in range(groups):
                sg = sinit_ref[0, g * hpg:(g + 1) * hpg].reshape(gw, state)
                st_ref[g] = sg.T
        else:
            st_ref[...] = jnp.zeros(st_ref.shape, F32)

    R = rows

    def pad_time(v):
        if R == L:
            return v
        return jnp.concatenate([v, jnp.zeros((L - R,) + v.shape[1:], v.dtype)], axis=0)

    win_ref[8:8 + R, :] = xbc_ref[...]

    ext = win_ref[0:8 + R, :]
    acc = cw_ref[0:1, :] * ext
    for k in range(1, SSD_CONV_W):
        acc = pltpu.roll(acc, 1, axis=0) + cw_ref[k:k + 1, :] * ext
    conv = acc[8:8 + R] + cb_ref[...]
    if R == L:
        win_ref[0:8, :] = win_ref[L:L + 8, :]
    xc = _silu(conv)

    dt = dt_ref[...] + dtb_ref[...]
    dt = jnp.maximum(dt, 0.0) + jnp.log(1.0 + jnp.exp(-jnp.abs(dt)))
    a = pad_time(dt * (-jnp.exp(alog_ref[...])))

    ri = lax.broadcasted_iota(jnp.int32, (L, L), 0)
    ci = lax.broadcasted_iota(jnp.int32, (L, L), 1)
    causal_full = ri >= ci
    tril = causal_full.astype(F32)
    a_cs_full = jnp.dot(tril, a, precision=lax.Precision.HIGHEST, preferred_element_type=F32)
    a_cst = a_cs_full.T
    a_cs = a_cs_full[:R]
    a_last = a_cs_full[L - 1:L, :]
    causal = causal_full[:R]

    e = e_ref[...]

    def expand(v):
        hi = v.astype(BF16)
        lo = (v - hi.astype(F32)).astype(BF16)
        return _dot(jnp.concatenate([hi, lo], axis=1), e)

    dt_x = expand(dt)
    din_x = expand(jnp.exp(a_cs))
    dend_x = expand(jnp.exp(a_last - a_cs))
    cdec_x = din_x[R - 1:R, :]

    rb = lax.broadcasted_iota(jnp.int32, (hpg * L, gw), 0) // L
    cbk = lax.broadcasted_iota(jnp.int32, (hpg * L, gw), 1) // headdim
    blockdiag = (rb == cbk).astype(BF16)

    def group_matmuls(g):
        bg = pad_time(xc[:, d_inner + g * state:d_inner + (g + 1) * state])
        cg16 = xc[:, d_inner + gn + g * state:d_inner + gn + (g + 1) * state].astype(BF16)
        st = st_ref[g]
        return bg, _dot_nt(cg16, bg.astype(BF16)), st, _dot(cg16, st.astype(BF16))

    ahead = group_matmuls(0)
    for g in range(groups):
        sl = slice(g * gw, (g + 1) * gw)
        bg, gmat, st, y_off = ahead
        if g + 1 < groups:
            ahead = group_matmuls(g + 1)
        xg = xc[:, sl]
        ms = []
        for hh in range(hpg):
            hd = g * hpg + hh
            seg = a_cs[:, hd:hd + 1] - a_cst[hd:hd + 1, :]
            decay = jnp.where(causal, jnp.exp(jnp.where(causal, seg, 0.0)), 0.0)
            ms.append((gmat * decay).astype(BF16))
        mcat = jnp.concatenate(ms, axis=1)
        xdt = xg * dt_x[:, sl]
        xdt16 = pad_time(xdt).astype(BF16)
        bd = jnp.concatenate([xdt16] * hpg, axis=0) * blockdiag
        y = _dot(mcat, bd) + y_off * din_x[:, sl] + dexp_ref[:, sl] * xg
        xd = pad_time(xdt * dend_x[:, sl]).astype(BF16)
        st_ref[g] = st * cdec_x[:, sl] + _dot(bg.T.astype(BF16), xd)
        yz = y * _silu(z_ref[:, sl])
        yn = yz * lax.rsqrt(jnp.mean(yz * yz, axis=-1, keepdims=True) + NORM_EPS) * nw_ref[:, sl]
        y_ref[:, sl] = yn.astype(y_ref.dtype)

    @pl.when(c == nc - 1)
    def _fin():
        if has_prev:
            out = sout_ref.at[0]
        else:
            out = sout_ref.at[layer, 0]
            for other in range(sout_ref.shape[0]):
                if other != layer:
                    sout_ref[other] = jnp.zeros(sout_ref.shape[1:], F32)
        for g in range(groups):
            out[g * hpg:(g + 1) * hpg] = st_ref[g].T.reshape(hpg, headdim, state)


def _ssd_scan(xbc, z, dt, row0, nseq, nchunks, rows, params, layer, n_layers, conv_init, ssm_init, state_out):
    cw, cb, dtb, alog, dexp, nw, e, dims = params
    heads, headdim, groups, state = dims
    d_inner = heads * headdim
    conv_dim = xbc.shape[1]
    has_init = conv_init is not None
    assert rows == SSD_CHUNK or nchunks == 1
    tok = lambda b, c: (row0 + b * nchunks + c, 0)
    out_tok = lambda b, c: (b * nchunks + c, 0)
    in_specs = [pl.BlockSpec((rows, conv_dim), tok),
                pl.BlockSpec((rows, d_inner), tok),
                pl.BlockSpec((rows, LANES), tok)]
    args = [xbc, z, dt]
    if has_init:
        in_specs += [pl.BlockSpec((None, 1) + conv_init.shape[2:], lambda b, c: (layer, b, 0, 0)),
                     pl.BlockSpec((None, 1) + ssm_init.shape[2:], lambda b, c: (layer, b, 0, 0, 0))]
        args += [conv_init, ssm_init]
    consts = [cw, cb, dtb, alog, dexp, nw, e]
    in_specs += [_const_spec(x.shape) for x in consts]
    args += consts
    aliases = {}
    if state_out is not None:
        aliases = {len(args): 1}
        in_specs.append(pl.BlockSpec(memory_space=pl.ANY))
        args.append(state_out)
        state_spec = pl.BlockSpec((None, 1, heads, headdim, state), lambda b, c: (layer, b, 0, 0, 0))
    else:
        state_spec = pl.BlockSpec((n_layers, 1, heads, headdim, state), lambda b, c: (0, b, 0, 0, 0))
    return pl.pallas_call(
        functools.partial(_ssd_scan_kernel, rows=rows, has_init=has_init, has_prev=state_out is not None,
                          layer=layer, heads=heads, headdim=headdim, groups=groups, state=state),
        grid=(nseq, nchunks),
        in_specs=in_specs,
        out_specs=[pl.BlockSpec((rows, d_inner), out_tok), state_spec],
        out_shape=[jax.ShapeDtypeStruct((nseq * nchunks * rows, d_inner), F32),
                   jax.ShapeDtypeStruct((n_layers, nseq, heads, headdim, state), F32)],
        scratch_shapes=[pltpu.VMEM((SSD_CHUNK + 8 + 8, conv_dim), F32),
                        pltpu.VMEM((groups, state, (heads // groups) * headdim), F32)],
        input_output_aliases=aliases,
        compiler_params=_cparams("parallel", "arbitrary"),
        name="ssd_scan_init" if has_init else "ssd_scan",
    )(*args)


def _out_proj_kernel(*refs, final, n_first, n_res_first):
    a_refs, refs = refs[:2], refs[2:]
    nres = 1 if n_res_first is None else 2
    w_ref, res_refs, refs = refs[0], refs[1:1 + nres], refs[1 + nres:]
    hnew = _read_rows(res_refs, n_res_first) + _dot(_read_rows(a_refs, n_first).astype(BF16), w_ref[...])
    if not final:
        refs[0][...] = hnew
        return
    fw_ref, o0_ref, o1_ref = refs
    hnew = _rms(hnew, fw_ref[...])
    i = pl.program_id(0)

    @pl.when(i < n_first)
    def _():
        o0_ref[...] = hnew

    @pl.when(i >= n_first)
    def _():
        o1_ref[...] = hnew


def _out_proj(a0, a1, w, res, final_w, tm):
    k = a0.shape[1]
    t = a0.shape[0] + a1.shape[0]
    d = w.shape[1]
    final = final_w is not None
    a_specs, a_args, n_first = _row_specs((a0, a1), tm)
    res_specs, res_args, n_res_first = _row_specs(res, tm)
    assert n_res_first in (None, n_first)
    first, second = a_specs[0].index_map, a_specs[1].index_map
    in_specs = a_specs + [_const_spec((k, d))] + res_specs
    args = a_args + [w] + res_args
    if final:
        in_specs.append(_const_spec((1, d)))
        args.append(final_w)
        out_specs = [pl.BlockSpec((tm, d), first), pl.BlockSpec((tm, d), second)]
        out_shape = [jax.ShapeDtypeStruct((a0.shape[0], d), F32), jax.ShapeDtypeStruct((a1.shape[0], d), F32)]
    else:
        out_specs = pl.BlockSpec((tm, d), lambda i: (i, 0))
        out_shape = jax.ShapeDtypeStruct((t, d), F32)
    return pl.pallas_call(
        functools.partial(_out_proj_kernel, final=final, n_first=n_first, n_res_first=n_res_first),
        grid=(t // tm,),
        in_specs=in_specs,
        out_specs=out_specs,
        out_shape=out_shape,
        compiler_params=_cparams("arbitrary" if final else "parallel"),
        name="out_proj_final" if final else "out_proj",
    )(*args)


def _rotate_half(x, first, half):
    n = x.shape[1]
    lane = lax.broadcasted_iota(jnp.int32, x.shape, 1) & (LANES - 1)
    from_right = pltpu.roll(x, n - half, axis=1)
    from_left = pltpu.roll(x, half, axis=1)
    return jnp.where(lane < first + half, -from_right, from_left)


def _mla_proj_kernel(*refs, prompt, heads, q_lora, kv_lora, rope, nope, scale):
    if prompt:
        (x_ref, nw_ref, win_ref, qnw_ref, kvnw_ref, wq_ref, cosq_ref, sinq_ref, cosk_ref, sink_ref,
         wk_ref, wvt_ref, ckv_ref, kpe_ref, gate_ref, q_ref, k_ref, vt_ref) = refs
    else:
        (x_ref, nw_ref, win_ref, qnw_ref, kvnw_ref, wq_ref, cosq_ref, sinq_ref, cosk_ref, sink_ref,
         wukt_ref, ckv_ref, kpe_ref, gate_ref, qlat_ref, qpe_ref) = refs
    xn = _rms(x_ref[...], nw_ref[...]).astype(BF16)
    o1 = q_lora
    o2 = o1 + kv_lora
    cq = _rms(_dot(xn, win_ref[:, 0:o1]), qnw_ref[...]).astype(BF16)
    ckv = _rms(_dot(xn, win_ref[:, o1:o2]), kvnw_ref[...])
    ckv_ref[...] = ckv
    kp = _dot(xn, win_ref[:, o2:o2 + LANES])
    kpe = kp * cosk_ref[...] + _rotate_half(kp, 0, rope // 2) * sink_ref[...]
    kpe_ref[...] = kpe[:, :rope]
    gate_ref[...] = _dot(xn, win_ref[:, o2 + LANES:])
    hc = 8
    cosq = jnp.concatenate([cosq_ref[...] * scale] * hc, axis=1)
    sinq = jnp.concatenate([sinq_ref[...] * scale] * hc, axis=1)
    for h0 in range(0, heads, hc):
        cs = slice(h0 * LANES, (h0 + hc) * LANES)
        qa = _dot(cq, wq_ref[:, cs])
        qc = qa * cosq + _rotate_half(qa, nope, rope // 2) * sinq
        if prompt:
            q_ref[:, cs] = qc.astype(BF16)
        else:
            for h in range(h0, h0 + hc):
                off = (h - h0) * LANES
                qlat_ref[h] = _dot(qc[:, off:off + nope].astype(BF16), wukt_ref[h])
                qpe_ref[h] = qc[:, off + nope:off + nope + rope]
    if prompt:
        ckv16 = ckv.astype(BF16)
        kin = jnp.concatenate([ckv16, kpe.astype(BF16)], axis=1)
        for c0 in range(0, heads * LANES, 1024):
            k_ref[:, c0:c0 + 1024] = _dot(kin, wk_ref[:, c0:c0 + 1024]).astype(BF16)
        vt = _dot_nt(wvt_ref[...], ckv16)
        vhead = vt.shape[0] // heads
        npad = vt_ref.shape[1] // heads - vhead
        pad = (lax.broadcasted_iota(jnp.int32, (npad, vt.shape[1]), 0) == 0).astype(F32)
        pieces = []
        for h in range(heads):
            pieces += [vt[h * vhead:(h + 1) * vhead], pad]
        vt_ref[0] = jnp.concatenate(pieces, axis=0).astype(BF16)


def _mla_proj(h_all, row0_blocks, ntok, tm, seq_blocks, nw, w, tables, dims, prompt):
    heads, q_lora, kv_lora, rope, nope, vhead, scale = dims
    d = h_all.shape[1]
    cosq, sinq, cosk, sink = tables
    tok = lambda i: (row0_blocks + i, 0)
    out_tok = lambda i: (i, 0)
    pos = lambda i: (i % seq_blocks, 0)
    shared = [w["in"], w["qn"], w["kvn"], w["q"]]
    in_specs = [pl.BlockSpec((tm, d), tok), _const_spec(nw.shape)] + [_const_spec(x.shape) for x in shared]
    in_specs += [pl.BlockSpec((tm, LANES), pos)] * 4
    args = [h_all, nw] + shared + [cosq, sinq, cosk, sink]
    gw = heads * vhead
    out_specs = [pl.BlockSpec((tm, kv_lora), out_tok), pl.BlockSpec((tm, rope), out_tok), pl.BlockSpec((tm, gw), out_tok)]
    out_shape = [jax.ShapeDtypeStruct((ntok, kv_lora), F32), jax.ShapeDtypeStruct((ntok, rope), F32),
                 jax.ShapeDtypeStruct((ntok, gw), F32)]
    if prompt:
        nb = ntok // (seq_blocks * tm)
        in_specs += [_const_spec(w["k"].shape), _const_spec(w["vt"].shape)]
        args += [w["k"], w["vt"]]
        vr = heads * _vt_rows(vhead)
        out_specs += [pl.BlockSpec((tm, heads * LANES), out_tok), pl.BlockSpec((tm, heads * LANES), out_tok),
                      pl.BlockSpec((1, vr, tm), lambda i: (i // seq_blocks, 0, i % seq_blocks))]
        out_shape += [jax.ShapeDtypeStruct((ntok, heads * LANES), BF16), jax.ShapeDtypeStruct((ntok, heads * LANES), BF16),
                      jax.ShapeDtypeStruct((nb, vr, seq_blocks * tm), BF16)]
    else:
        in_specs += [_const_spec(w["ukt"].shape)]
        args += [w["ukt"]]
        out_specs += [pl.BlockSpec((heads, tm, kv_lora), lambda i: (0, i, 0)),
                      pl.BlockSpec((heads, tm, rope), lambda i: (0, i, 0))]
        out_shape += [jax.ShapeDtypeStruct((heads, ntok, kv_lora), F32), jax.ShapeDtypeStruct((heads, ntok, rope), F32)]
    return pl.pallas_call(
        functools.partial(_mla_proj_kernel, prompt=prompt, heads=heads, q_lora=q_lora, kv_lora=kv_lora, rope=rope,
                          nope=nope, scale=scale),
        grid=(ntok // tm,),
        in_specs=in_specs,
        out_specs=out_specs,
        out_shape=out_shape,
        compiler_params=_cparams("parallel"),
        name="mla_proj_prompt" if prompt else "mla_proj_sample",
    )(*args)


def _vt_rows(vhead):
    bf16_sublanes = 16
    return -(-(vhead + 1) // bf16_sublanes) * bf16_sublanes


def _prompt_attn_kernel(qi_ref, ki_ref, q_ref, k_ref, vt_ref, gate_ref, o_ref, m_ref, acc_ref, *,
                        heads, vhead, tile):
    p = pl.program_id(1)
    qi = qi_ref[p]
    ki = ki_ref[p]
    vrows = _vt_rows(vhead)

    @pl.when(ki == 0)
    def _init():
        m_ref[...] = jnp.full(m_ref.shape, NEG_INF, F32)
        acc_ref[...] = jnp.zeros(acc_ref.shape, F32)

    def step(diagonal):
        if diagonal:
            visible = (lax.broadcasted_iota(jnp.int32, (tile, tile), 0)
                       <= lax.broadcasted_iota(jnp.int32, (tile, tile), 1))

        def scores(h):
            sl = slice(h * LANES, (h + 1) * LANES)
            st = _dot_nt(k_ref[:, sl], q_ref[:, sl])
            return jnp.where(visible, st, NEG_INF) if diagonal else st

        def new_max(h, st):
            m_prev = m_ref[h]
            return m_prev, jnp.maximum(m_prev, jnp.max(st, axis=0, keepdims=True))

        def accumulate(h, st, m_prev, m_cur):
            alpha = jnp.exp2(m_prev - m_cur)
            pt = jnp.exp2(st - m_cur).astype(BF16)
            acc_ref[h] = acc_ref[h] * alpha + _dot(vt_ref[0, h * vrows:(h + 1) * vrows, :], pt)
            m_ref[h] = m_cur

        sts = {h: scores(h) for h in range(min(SCORE_AHEAD + 1, heads))}
        maxes = {0: new_max(0, sts[0])}
        for h in range(heads):
            if h + SCORE_AHEAD + 1 < heads:
                sts[h + SCORE_AHEAD + 1] = scores(h + SCORE_AHEAD + 1)
            if h + 1 < heads:
                maxes[h + 1] = new_max(h + 1, sts[h + 1])
            accumulate(h, sts.pop(h), *maxes.pop(h))

    pl.when(ki < qi)(lambda: step(False))

    @pl.when(ki == qi)
    def _last():
        step(True)
        ot = jnp.concatenate([acc_ref[h, :vhead, :] / acc_ref[h, vhead:vhead + 1, :] for h in range(heads)], axis=0)
        o_ref[...] = ot.T * _silu(gate_ref[...])


def _prompt_attn(q, k, vt, gate, nb, seq, heads, vhead, tile):
    nq = seq // tile
    pairs = [(a, b) for a in range(nq) for b in range(a + 1)]
    qi_tab = jnp.asarray([a for a, _ in pairs], jnp.int32)
    ki_tab = jnp.asarray([b for _, b in pairs], jnp.int32)
    vrows = _vt_rows(vhead)
    grid_spec = pltpu.PrefetchScalarGridSpec(
        num_scalar_prefetch=2,
        grid=(nb, len(pairs)),
        in_specs=[pl.BlockSpec((tile, heads * LANES), lambda b, p, qt, kt: (b * nq + qt[p], 0)),
                  pl.BlockSpec((tile, heads * LANES), lambda b, p, qt, kt: (b * nq + kt[p], 0)),
                  pl.BlockSpec((1, heads * vrows, tile), lambda b, p, qt, kt: (b, 0, kt[p])),
                  pl.BlockSpec((tile, heads * vhead), lambda b, p, qt, kt: (b * nq + qt[p], 0))],
        out_specs=pl.BlockSpec((tile, heads * vhead), lambda b, p, qt, kt: (b * nq + qt[p], 0)),
        scratch_shapes=[pltpu.VMEM((heads, 1, tile), F32),
                        pltpu.VMEM((heads, vrows, tile), F32)])
    return pl.pallas_call(
        functools.partial(_prompt_attn_kernel, heads=heads, vhead=vhead, tile=tile),
        grid_spec=grid_spec,
        out_shape=jax.ShapeDtypeStruct((nb * seq, heads * vhead), F32),
        compiler_params=_cparams("parallel", "arbitrary"),
        name="mla_prompt_attn",
    )(qi_tab, ki_tab, q, k, vt, gate)


def _sample_attn_kernel(pt_ref, qlat_ref, qpe_ref, ckvn_ref, kpen_ref, ckv_hbm, kpe_hbm, o_ref,
                        kbuf, pbuf, sem, m_ref, l_ref, acc_ref, *, layer, npages, heads, tnew, page, nchains):
    b = pl.program_id(0)
    s = pl.program_id(1)
    nb = pl.num_programs(0)
    ns = pl.num_programs(1)
    t = b * ns + s
    slot = lax.rem(t, 2)
    rows = heads * tnew

    def page_copies(bb, ss, sl):
        out = []
        for i in range(npages):
            pid = pt_ref[bb, ss * npages + i]
            out.append(pltpu.make_async_copy(ckv_hbm.at[layer, pid], kbuf.at[sl, pl.ds(i * page, page)], sem.at[0, sl]))
            out.append(pltpu.make_async_copy(kpe_hbm.at[layer, pid], pbuf.at[sl, :, pl.ds(i * page, page)], sem.at[1, sl]))
        return out

    @pl.when(t == 0)
    def _first():
        for c in page_copies(b, s, slot):
            c.start()

    @pl.when(t + 1 < nb * ns)
    def _prefetch():
        wrap = s + 1 == ns
        for c in page_copies(jnp.where(wrap, b + 1, b), jnp.where(wrap, 0, s + 1), 1 - slot):
            c.start()

    q = qlat_ref[...].reshape(rows, qlat_ref.shape[2]).astype(BF16)
    qp = qpe_ref[...].reshape(rows, qpe_ref.shape[2]).astype(BF16)

    @pl.when(s == 0)
    def _new_tokens():
        cn = ckvn_ref[...].astype(BF16)
        sc = _dot_nt(q, cn) + _dot_nt(qp, kpen_ref[...].astype(BF16))
        tq_ = lax.broadcasted_iota(jnp.int32, (rows, tnew), 0) % tnew
        tk_ = lax.broadcasted_iota(jnp.int32, (rows, tnew), 1)
        sc = jnp.where(tk_ <= tq_, sc, NEG_INF)
        m = jnp.max(sc, axis=-1, keepdims=True)
        pr = jnp.exp2(sc - m)
        m_ref[0] = m
        l_ref[0] = jnp.sum(pr, axis=-1, keepdims=True)
        acc_ref[0] = _dot(pr, ckvn_ref[...])
        for c in range(1, nchains):
            m_ref[c] = jnp.full((rows, 1), NEG_INF, F32)
            l_ref[c] = jnp.zeros((rows, 1), F32)
            acc_ref[c] = jnp.zeros(acc_ref.shape[1:], F32)

    for c in page_copies(b, s, slot):
        c.wait()

    nkeys = npages * page // nchains
    cks, scs = [], []
    for c in range(nchains):
        ck = kbuf[slot, pl.ds(c * nkeys, nkeys), :].astype(BF16)
        kp = pbuf[slot, :, pl.ds(c * nkeys, nkeys)].astype(BF16)
        cks.append(ck)
        scs.append(_dot_nt(q, ck) + _dot(qp, kp))
    for c in range(nchains):
        m_prev = m_ref[c]
        m_cur = jnp.maximum(m_prev, jnp.max(scs[c], axis=-1, keepdims=True))
        alpha = jnp.exp2(m_prev - m_cur)
        pr = jnp.exp2(scs[c] - m_cur)
        l_ref[c] = alpha * l_ref[c] + jnp.sum(pr, axis=-1, keepdims=True)
        acc_ref[c] = acc_ref[c] * alpha + _dot(pr.astype(BF16), cks[c])
        m_ref[c] = m_cur

    @pl.when(s == ns - 1)
    def _fin():
        m = m_ref[0]
        for c in range(1, nchains):
            m = jnp.maximum(m, m_ref[c])
        l = jnp.zeros((rows, 1), F32)
        acc = jnp.zeros(acc_ref.shape[1:], F32)
        for c in range(nchains):
            w = jnp.exp2(m_ref[c] - m)
            l = l + l_ref[c] * w
            acc = acc + acc_ref[c] * w
        o_ref[0] = acc / l


def _sample_attn(page_table, qlat, qpe, ckv_new, kpe_new, cache_ckv, cache_kpe_t, layer, npages, nchains):
    heads, ntok, kv_lora = qlat.shape
    rope = qpe.shape[2]
    nb, n_pages_total = page_table.shape
    tnew = ntok // nb
    page = cache_ckv.shape[2]
    steps = n_pages_total // npages
    rows = heads * tnew
    in_specs = [pl.BlockSpec((heads, tnew, kv_lora), lambda b, s, pt: (0, b, 0)),
                pl.BlockSpec((heads, tnew, rope), lambda b, s, pt: (0, b, 0)),
                pl.BlockSpec((tnew, kv_lora), lambda b, s, pt: (b, 0)),
                pl.BlockSpec((tnew, rope), lambda b, s, pt: (b, 0)),
                pl.BlockSpec(memory_space=pl.ANY),
                pl.BlockSpec(memory_space=pl.ANY)]
    grid_spec = pltpu.PrefetchScalarGridSpec(
        num_scalar_prefetch=1,
        grid=(nb, steps),
        in_specs=in_specs,
        out_specs=pl.BlockSpec((1, rows, kv_lora), lambda b, s, pt: (b, 0, 0)),
        scratch_shapes=[pltpu.VMEM((2, npages * page, kv_lora), F32),
                        pltpu.VMEM((2, rope, npages * page), F32),
                        pltpu.SemaphoreType.DMA((2, 2)),
                        pltpu.VMEM((nchains, rows, 1), F32), pltpu.VMEM((nchains, rows, 1), F32),
                        pltpu.VMEM((nchains, rows, kv_lora), F32)])
    return pl.pallas_call(
        functools.partial(_sample_attn_kernel, layer=layer, npages=npages, heads=heads, tnew=tnew, page=page,
                          nchains=nchains),
        grid_spec=grid_spec,
        out_shape=jax.ShapeDtypeStruct((nb, rows, kv_lora), F32),
        compiler_params=_cparams("arbitrary", "arbitrary"),
        name="mla_sample_attn",
    )(page_table, qlat, qpe, ckv_new, kpe_new, cache_ckv, cache_kpe_t)


def _sample_post_kernel(ol_ref, wuv_ref, gate_ref, o_ref, *, heads, tnew, vhead):
    bt = ol_ref.shape[0]
    for h in range(heads):
        x = ol_ref[:, h * tnew:(h + 1) * tnew, :].reshape(bt * tnew, ol_ref.shape[2]).astype(BF16)
        g = gate_ref[:, h * vhead:(h + 1) * vhead]
        o_ref[:, h * vhead:(h + 1) * vhead] = _dot(x, wuv_ref[h]) * _silu(g)


def _sample_post(olat, wuv, gate, tnew, bt):
    nb, rows, kv_lora = olat.shape
    heads, _, vhead = wuv.shape
    ntok = nb * tnew
    return pl.pallas_call(
        functools.partial(_sample_post_kernel, heads=heads, tnew=tnew, vhead=vhead),
        grid=(nb // bt,),
        in_specs=[pl.BlockSpec((bt, rows, kv_lora), lambda i: (i, 0, 0)),
                  _const_spec(wuv.shape),
                  pl.BlockSpec((bt * tnew, heads * vhead), lambda i: (i, 0))],
        out_specs=pl.BlockSpec((bt * tnew, heads * vhead), lambda i: (i, 0)),
        out_shape=jax.ShapeDtypeStruct((ntok, heads * vhead), F32),
        compiler_params=_cparams("parallel"),
        name="mla_sample_post",
    )(olat, wuv, gate)


def _rope_tables(pos, rope, nope):
    inv = 1.0 / (ROPE_BASE ** (jnp.arange(0, rope, 2, dtype=F32) / rope))
    ang = pos[:, None] * inv[None, :]
    cos, sin = jnp.cos(ang), jnp.sin(ang)
    n = pos.shape[0]
    cc = jnp.concatenate([cos, cos], axis=1)
    ss = jnp.concatenate([sin, sin], axis=1)
    cosk = jnp.pad(cc, ((0, 0), (0, LANES - rope)))
    sink = jnp.pad(ss, ((0, 0), (0, LANES - rope)))
    cosq = jnp.concatenate([jnp.ones((n, nope), F32), cc, jnp.zeros((n, LANES - nope - rope), F32)], axis=1)
    sinq = jnp.concatenate([jnp.zeros((n, nope), F32), ss, jnp.zeros((n, LANES - nope - rope), F32)], axis=1)
    return cosq, sinq, cosk, sink


def _mla_weights(w_in, qn, kvn, w_uq, w_uk, w_uv, q_lora, kv_lora, rope, nope):
    heads = w_uq.shape[1]
    vhead = w_uv.shape[2]
    o2, o3 = q_lora + kv_lora, q_lora + kv_lora + rope
    w_in_big = jnp.concatenate([w_in[:, :o2], jnp.pad(w_in[:, o2:o3], ((0, 0), (0, LANES - rope))), w_in[:, o3:]], axis=1)
    zpad = jnp.zeros((q_lora, heads, LANES - nope - rope), F32)
    wq = jnp.concatenate([w_uq, zpad], axis=2).reshape(q_lora, heads * LANES)
    wk_lat = jnp.concatenate([w_uk, jnp.zeros((kv_lora, heads, LANES - nope), F32)], axis=2).reshape(kv_lora, heads * LANES)
    place = jnp.zeros((LANES, heads, LANES), F32).at[jnp.arange(rope), :, nope + jnp.arange(rope)].set(1.0)
    wk = jnp.concatenate([wk_lat, place.reshape(LANES, heads * LANES)], axis=0)
    wvt = w_uv.transpose(1, 2, 0).reshape(heads * vhead, kv_lora)
    return {"in": w_in_big.astype(BF16), "qn": qn.reshape(1, -1), "kvn": kvn.reshape(1, -1),
            "q": wq.astype(BF16), "k": wk.astype(BF16), "vt": wvt.astype(BF16),
            "ukt": w_uk.transpose(1, 2, 0).astype(BF16), "uv": w_uv.transpose(1, 0, 2).astype(BF16)}


def kernel(x_prompt, x_sample, state_ssm, state_conv, cache_ckv, cache_kpe, page_table, norm_w, final_norm_w,
           ssd_w_in, ssd_conv_w, ssd_conv_b, ssd_dt_bias, ssd_a_log, ssd_d, ssd_norm_w, ssd_w_out,
           mla_w_in, mla_q_norm_w, mla_kv_norm_w, mla_w_uq, mla_w_uk, mla_w_uv, mla_w_out):
    nb, seq, d = x_prompt.shape
    db, dseq, _ = x_sample.shape
    depth = norm_w.shape[0]
    tp, ts = nb * seq, db * dseq
    _, _, heads_s, headdim, state = state_ssm.shape
    d_inner = heads_s * headdim
    conv_dim = ssd_conv_w.shape[2]
    gn = (conv_dim - d_inner) // 2
    groups = gn // state
    q_lora, heads_m, qk = mla_w_uq.shape[1:]
    kv_lora = mla_w_uk.shape[1]
    nope = mla_w_uk.shape[3]
    rope = qk - nope
    vhead = mla_w_uv.shape[3]
    past = page_table.shape[1] * cache_ckv.shape[2]
    scale = (nope + rope) ** -0.5 * LOG2E

    assert seq % SSD_CHUNK == 0 and dseq <= SSD_CHUNK and tp % dseq == 0
    tm = 512
    assert tp % tm == 0 and ts % tm == 0

    h_all = (x_prompt.reshape(tp, d), x_sample.reshape(ts, d))

    e = (jnp.arange(d_inner)[None, :] // headdim == jnp.arange(LANES)[:, None]).astype(BF16)
    e = jnp.concatenate([e, e], axis=0)
    tab_p = _rope_tables(jnp.arange(seq, dtype=F32), rope, nope)
    tab_s = _rope_tables(jnp.tile(past + jnp.arange(dseq, dtype=F32), ts // dseq), rope, nope)
    mla_dims = (heads_m, q_lora, kv_lora, rope, nope, vhead, scale)

    cache_kpe_t = jnp.swapaxes(cache_kpe, 2, 3)
    n_ssd = (depth + 1) // 2
    tail = SSD_CONV_W - 1

    p_conv, p_ckv, p_kpe = [], [], []
    s_conv, s_ckv, s_kpe = [], [], []
    p_ssm = s_ssm = None
    for i in range(depth):
        j = i // 2
        nw = norm_w[i].reshape(1, d)
        fin_w = final_norm_w.reshape(1, d) if i == depth - 1 else None
        if i % 2 == 0:
            w_big = jnp.pad(ssd_w_in[j], ((0, 0), (0, LANES - heads_s))).astype(BF16)
            z, xbc, dt = _ssd_in_proj(h_all, nw, w_big, d_inner, conv_dim, tm)
            params = (ssd_conv_w[j], ssd_conv_b[j].reshape(1, -1),
                      jnp.pad(ssd_dt_bias[j], (0, LANES - heads_s)).reshape(1, LANES),
                      jnp.pad(ssd_a_log[j], (0, LANES - heads_s)).reshape(1, LANES),
                      jnp.repeat(ssd_d[j], headdim).reshape(1, d_inner),
                      ssd_norm_w[j].reshape(1, d_inner), e, (heads_s, headdim, groups, state))
            y_p, p_ssm = _ssd_scan(xbc, z, dt, 0, nb, seq // SSD_CHUNK, SSD_CHUNK, params, j, n_ssd,
                                   None, None, p_ssm)
            y_s, s_ssm = _ssd_scan(xbc, z, dt, tp // dseq, db, 1, dseq, params, j, n_ssd,
                                   state_conv, state_ssm, s_ssm)
            h_all = _out_proj(y_p, y_s, ssd_w_out[j].astype(BF16), h_all, fin_w, tm)
            p_conv.append(jnp.stack([xbc[(b + 1) * seq - tail:(b + 1) * seq] for b in range(nb)]))
            s_conv.append(xbc[tp:].reshape(db, dseq, conv_dim)[:, dseq - tail:])
        else:
            w = _mla_weights(mla_w_in[j], mla_q_norm_w[j], mla_kv_norm_w[j], mla_w_uq[j], mla_w_uk[j], mla_w_uv[j],
                             q_lora, kv_lora, rope, nope)
            tmp = 256
            ckv_p, kpe_p, gate_p, q_p, k_p, vt_p = _mla_proj(h_all, 0, tp, tmp, seq // tmp, nw, w, tab_p, mla_dims, True)
            ckv_s, kpe_s, gate_s, qlat_s, qpe_s = _mla_proj(h_all, tp // tmp, ts, tmp, ts // tmp, nw, w, tab_s,
                                                            mla_dims, False)
            og_p = _prompt_attn(q_p, k_p, vt_p, gate_p, nb, seq, heads_m, vhead, math.gcd(ATTN_TILE, seq))
            npg = math.gcd(PAGES_PER_STEP, page_table.shape[1])
            olat = _sample_attn(page_table, qlat_s, qpe_s, ckv_s, kpe_s, cache_ckv, cache_kpe_t, j, npg, DECODE_CHAINS)
            og_s = _sample_post(olat, w["uv"], gate_s, dseq, math.gcd(16, db))
            h_all = _out_proj(og_p, og_s, mla_w_out[j].astype(BF16), h_all, fin_w, tm)
            p_ckv.append(ckv_p.reshape(nb, seq, kv_lora))
            p_kpe.append(kpe_p.reshape(nb, seq, rope))
            s_ckv.append(ckv_s.reshape(db, dseq, kv_lora))
            s_kpe.append(kpe_s.reshape(db, dseq, rope))
    y_prompt = h_all[0].reshape(nb, seq, d)
    y_sample = h_all[1].reshape(db, dseq, d)
    return (y_prompt, y_sample,
            p_ssm, jnp.stack(p_conv), jnp.stack(p_ckv), jnp.stack(p_kpe),
            s_ssm, jnp.stack(s_conv), jnp.stack(s_ckv), jnp.stack(s_kpe))
```

```python
import functools
import math

import jax
import jax.numpy as jnp
from jax import lax
from jax.experimental import pallas as pl
from jax.experimental.pallas import tpu as pltpu

F32 = jnp.float32
BF16 = jnp.bfloat16

NORM_EPS = 1e-6
NEG_INF = -1e30
ROPE_BASE = 10000.0
LOG2E = 1.4426950408889634

LANES = 128
VMEM_LIMIT_BYTES = 56 * 1024 * 1024

SSD_CHUNK = 128
SSD_CONV_W = 4
PAGES_PER_STEP = 64
ATTN_TILE = 512
DECODE_CHAINS = 4
SCORE_AHEAD = 1


def _cparams(*sem):
    return pltpu.CompilerParams(dimension_semantics=sem, vmem_limit_bytes=VMEM_LIMIT_BYTES)


def _const_spec(shape):
    nd = len(shape)
    return pl.BlockSpec(shape, lambda *_: (0,) * nd, pipeline_mode=pl.Buffered(1))


def _silu(x):
    h = 0.5 * x
    return h * jnp.tanh(h) + h


def _dot(a, b):
    return jnp.dot(a, b, preferred_element_type=F32)


def _dot_nt(a, b):
    return lax.dot_general(a, b, (((1,), (1,)), ((), ())), preferred_element_type=F32)


def _rms(x, w):
    return x * lax.rsqrt(jnp.mean(x * x, axis=-1, keepdims=True) + NORM_EPS) * w


def _row_specs(h, tm):
    if not isinstance(h, tuple):
        return [pl.BlockSpec((tm, h.shape[1]), lambda i: (i, 0))], [h], None
    n_first = h[0].shape[0] // tm
    width = h[0].shape[1]
    return ([pl.BlockSpec((tm, width), lambda i: (jnp.minimum(i, n_first - 1), 0)),
             pl.BlockSpec((tm, width), lambda i: (jnp.maximum(i - n_first, 0), 0))], list(h), n_first)


def _read_rows(refs, n_first):
    if n_first is None:
        return refs[0][...]
    return jnp.where(pl.program_id(0) < n_first, refs[0][...], refs[1][...])


def _ssd_in_kernel(*refs, n_first, d_inner, conv_dim, n_chunk):
    nx = 1 if n_first is None else 2
    nw_ref, w_ref, z_ref, xbc_ref, dt_ref = refs[nx:]
    xn = _rms(_read_rows(refs[:nx], n_first), nw_ref[...]).astype(BF16)
    col = 0
    for o_ref, width in ((z_ref, d_inner), (xbc_ref, conv_dim), (dt_ref, LANES)):
        for c0 in range(0, width, n_chunk):
            c1 = min(c0 + n_chunk, width)
            o_ref[:, c0:c1] = _dot(xn, w_ref[:, col + c0:col + c1]).astype(o_ref.dtype)
        col += width


def _ssd_in_proj(h, norm_w, w_big, d_inner, conv_dim, tm):
    x_specs, x_args, n_first = _row_specs(h, tm)
    t = sum(x.shape[0] for x in x_args)
    d, n = w_big.shape
    return pl.pallas_call(
        functools.partial(_ssd_in_kernel, n_first=n_first, d_inner=d_inner, conv_dim=conv_dim, n_chunk=1024),
        grid=(t // tm,),
        in_specs=x_specs + [_const_spec((1, d)), _const_spec((d, n))],
        out_specs=[pl.BlockSpec((tm, d_inner), lambda i: (i, 0)),
                   pl.BlockSpec((tm, conv_dim), lambda i: (i, 0)),
                   pl.BlockSpec((tm, LANES), lambda i: (i, 0))],
        out_shape=[jax.ShapeDtypeStruct((t, d_inner), F32),
                   jax.ShapeDtypeStruct((t, conv_dim), F32),
                   jax.ShapeDtypeStruct((t, LANES), F32)],
        compiler_params=_cparams("parallel"),
        name="ssd_in_proj",
    )(*x_args, norm_w, w_big)


def _ssd_scan_kernel(*refs, rows, has_init, has_prev, layer, heads, headdim, groups, state):
    if has_prev:
        refs = refs[:-5] + refs[-4:]
    if has_init:
        (xbc_ref, z_ref, dt_ref, cinit_ref, sinit_ref, cw_ref, cb_ref, dtb_ref, alog_ref, dexp_ref,
         nw_ref, e_ref, y_ref, sout_ref, win_ref, st_ref) = refs
    else:
        (xbc_ref, z_ref, dt_ref, cw_ref, cb_ref, dtb_ref, alog_ref, dexp_ref,
         nw_ref, e_ref, y_ref, sout_ref, win_ref, st_ref) = refs
    L = SSD_CHUNK
    d_inner = heads * headdim
    hpg = heads // groups
    gw = hpg * headdim
    gn = groups * state
    c = pl.program_id(1)
    nc = pl.num_programs(1)

    @pl.when(c == 0)
    def _init():
        win_ref[0:8, :] = jnp.zeros((8, win_ref.shape[1]), F32)
        if has_init:
            win_ref[8 - (SSD_CONV_W - 1):8, :] = cinit_ref[0]
            for g in range(groups):
                sg = sinit_ref[0, g * hpg:(g + 1) * hpg].reshape(gw, state)
                st_ref[g] = sg.T
        else:
            st_ref[...] = jnp.zeros(st_ref.shape, F32)

    R = rows

    def pad_time(v):
        if R == L:
            return v
        return jnp.concatenate([v, jnp.zeros((L - R,) + v.shape[1:], v.dtype)], axis=0)

    win_ref[8:8 + R, :] = xbc_ref[...]

    ext = win_ref[0:8 + R, :]
    acc = cw_ref[0:1, :] * ext
    for k in range(1, SSD_CONV_W):
        acc = pltpu.roll(acc, 1, axis=0) + cw_ref[k:k + 1, :] * ext
    conv = acc[8:8 + R] + cb_ref[...]
    if R == L:
        win_ref[0:8, :] = win_ref[L:L + 8, :]
    xc = _silu(conv)

    dt = dt_ref[...] + dtb_ref[...]
    dt = jnp.maximum(dt, 0.0) + jnp.log(1.0 + jnp.exp(-jnp.abs(dt)))
    a = pad_time(dt * (-jnp.exp(alog_ref[...])))

    ri = lax.broadcasted_iota(jnp.int32, (L, L), 0)
    ci = lax.broadcasted_iota(jnp.int32, (L, L), 1)
    causal_full = ri >= ci
    tril = causal_full.astype(F32)
    a_cs_full = jnp.dot(tril, a, precision=lax.Precision.HIGHEST, preferred_element_type=F32)
    a_cst = a_cs_full.T
    a_cs = a_cs_full[:R]
    a_last = a_cs_full[L - 1:L, :]
    causal = causal_full[:R]

    e = e_ref[...]

    def expand(v):
        hi = v.astype(BF16)
        lo = (v - hi.astype(F32)).astype(BF16)
        return _dot(jnp.concatenate([hi, lo], axis=1), e)

    dt_x = expand(dt)
    din_x = expand(jnp.exp(a_cs))
    dend_x = expand(jnp.exp(a_last - a_cs))
    cdec_x = din_x[R - 1:R, :]

    rb = lax.broadcasted_iota(jnp.int32, (hpg * L, gw), 0) // L
    cbk = lax.broadcasted_iota(jnp.int32, (hpg * L, gw), 1) // headdim
    blockdiag = (rb == cbk).astype(BF16)

    def group_matmuls(g):
        bg = pad_time(xc[:, d_inner + g * state:d_inner + (g + 1) * state])
        cg16 = xc[:, d_inner + gn + g * state:d_inner + gn + (g + 1) * state].astype(BF16)
        st = st_ref[g]
        decays = []
        for hh in range(hpg):
            hd = g * hpg + hh
            seg = a_cs[:, hd:hd + 1] - a_cst[hd:hd + 1, :]
            decays.append(jnp.where(causal, jnp.exp(jnp.where(causal, seg, 0.0)), 0.0))
        return bg, _dot_nt(cg16, bg.astype(BF16)), st, _dot(cg16, st.astype(BF16)), decays

    ahead = group_matmuls(0)
    for g in range(groups):
        sl = slice(g * gw, (g + 1) * gw)
        bg, gmat, st, y_off, decays = ahead
        if g + 1 < groups:
            ahead = group_matmuls(g + 1)
        xg = xc[:, sl]
        ms = [(gmat * decay).astype(BF16) for decay in decays]
        mcat = jnp.concatenate(ms, axis=1)
        xdt = xg * dt_x[:, sl]
        xdt16 = pad_time(xdt).astype(BF16)
        bd = jnp.concatenate([xdt16] * hpg, axis=0) * blockdiag
        y = _dot(mcat, bd) + y_off * din_x[:, sl] + dexp_ref[:, sl] * xg
        xd = pad_time(xdt * dend_x[:, sl]).astype(BF16)
        st_ref[g] = st * cdec_x[:, sl] + _dot(bg.T.astype(BF16), xd)
        yz = y * _silu(z_ref[:, sl])
        yn = yz * lax.rsqrt(jnp.mean(yz * yz, axis=-1, keepdims=True) + NORM_EPS) * nw_ref[:, sl]
        y_ref[:, sl] = yn.astype(y_ref.dtype)

    @pl.when(c == nc - 1)
    def _fin():
        if has_prev:
            out = sout_ref.at[0]
        else:
            out = sout_ref.at[layer, 0]
            for other in range(sout_ref.shape[0]):
                if other != layer:
                    sout_ref[other] = jnp.zeros(sout_ref.shape[1:], F32)
        for g in range(groups):
            out[g * hpg:(g + 1) * hpg] = st_ref[g].T.reshape(hpg, headdim, state)


def _ssd_scan(xbc, z, dt, row0, nseq, nchunks, rows, params, layer, n_layers, conv_init, ssm_init, state_out):
    cw, cb, dtb, alog, dexp, nw, e, dims = params
    heads, headdim, groups, state = dims
    d_inner = heads * headdim
    conv_dim = xbc.shape[1]
    has_init = conv_init is not None
    assert rows == SSD_CHUNK or nchunks == 1
    tok = lambda b, c: (row0 + b * nchunks + c, 0)
    out_tok = lambda b, c: (b * nchunks + c, 0)
    in_specs = [pl.BlockSpec((rows, conv_dim), tok),
                pl.BlockSpec((rows, d_inner), tok),
                pl.BlockSpec((rows, LANES), tok)]
    args = [xbc, z, dt]
    if has_init:
        in_specs += [pl.BlockSpec((None, 1) + conv_init.shape[2:], lambda b, c: (layer, b, 0, 0)),
                     pl.BlockSpec((None, 1) + ssm_init.shape[2:], lambda b, c: (layer, b, 0, 0, 0))]
        args += [conv_init, ssm_init]
    consts = [cw, cb, dtb, alog, dexp, nw, e]
    in_specs += [_const_spec(x.shape) for x in consts]
    args += consts
    aliases = {}
    if state_out is not None:
        aliases = {len(args): 1}
        in_specs.append(pl.BlockSpec(memory_space=pl.ANY))
        args.append(state_out)
        state_spec = pl.BlockSpec((None, 1, heads, headdim, state), lambda b, c: (layer, b, 0, 0, 0))
    else:
        state_spec = pl.BlockSpec((n_layers, 1, heads, headdim, state), lambda b, c: (0, b, 0, 0, 0))
    return pl.pallas_call(
        functools.partial(_ssd_scan_kernel, rows=rows, has_init=has_init, has_prev=state_out is not None,
                          layer=layer, heads=heads, headdim=headdim, groups=groups, state=state),
        grid=(nseq, nchunks),
        in_specs=in_specs,
        out_specs=[pl.BlockSpec((rows, d_inner), out_tok), state_spec],
        out_shape=[jax.ShapeDtypeStruct((nseq * nchunks * rows, d_inner), F32),
                   jax.ShapeDtypeStruct((n_layers, nseq, heads, headdim, state), F32)],
        scratch_shapes=[pltpu.VMEM((SSD_CHUNK + 8 + 8, conv_dim), F32),
                        pltpu.VMEM((groups, state, (heads // groups) * headdim), F32)],
        input_output_aliases=aliases,
        compiler_params=_cparams("parallel", "arbitrary"),
        name="ssd_scan_init" if has_init else "ssd_scan",
    )(*args)


def _out_proj_kernel(*refs, final, n_first, n_res_first):
    a_refs, refs = refs[:2], refs[2:]
    nres = 1 if n_res_first is None else 2
    w_ref, res_refs, refs = refs[0], refs[1:1 + nres], refs[1 + nres:]
    hnew = _read_rows(res_refs, n_res_first) + _dot(_read_rows(a_refs, n_first).astype(BF16), w_ref[...])
    if not final:
        refs[0][...] = hnew
        return
    fw_ref, o0_ref, o1_ref = refs
    hnew = _rms(hnew, fw_ref[...])
    i = pl.program_id(0)

    @pl.when(i < n_first)
    def _():
        o0_ref[...] = hnew

    @pl.when(i >= n_first)
    def _():
        o1_ref[...] = hnew


def _out_proj(a0, a1, w, res, final_w, tm):
    k = a0.shape[1]
    t = a0.shape[0] + a1.shape[0]
    d = w.shape[1]
    final = final_w is not None
    a_specs, a_args, n_first = _row_specs((a0, a1), tm)
    res_specs, res_args, n_res_first = _row_specs(res, tm)
    assert n_res_first in (None, n_first)
    first, second = a_specs[0].index_map, a_specs[1].index_map
    in_specs = a_specs + [_const_spec((k, d))] + res_specs
    args = a_args + [w] + res_args
    if final:
        in_specs.append(_const_spec((1, d)))
        args.append(final_w)
        out_specs = [pl.BlockSpec((tm, d), first), pl.BlockSpec((tm, d), second)]
        out_shape = [jax.ShapeDtypeStruct((a0.shape[0], d), F32), jax.ShapeDtypeStruct((a1.shape[0], d), F32)]
    else:
        out_specs = pl.BlockSpec((tm, d), lambda i: (i, 0))
        out_shape = jax.ShapeDtypeStruct((t, d), F32)
    return pl.pallas_call(
        functools.partial(_out_proj_kernel, final=final, n_first=n_first, n_res_first=n_res_first),
        grid=(t // tm,),
        in_specs=in_specs,
        out_specs=out_specs,
        out_shape=out_shape,
        compiler_params=_cparams("arbitrary" if final else "parallel"),
        name="out_proj_final" if final else "out_proj",
    )(*args)


def _rotate_half(x, first, half):
    n = x.shape[1]
    lane = lax.broadcasted_iota(jnp.int32, x.shape, 1) & (LANES - 1)
    from_right = pltpu.roll(x, n - half, axis=1)
    from_left = pltpu.roll(x, half, axis=1)
    return jnp.where(lane < first + half, -from_right, from_left)


def _mla_proj_kernel(*refs, prompt, heads, q_lora, kv_lora, rope, nope, scale):
    if prompt:
        (x_ref, nw_ref, win_ref, qnw_ref, kvnw_ref, wq_ref, cosq_ref, sinq_ref, cosk_ref, sink_ref,
         wk_ref, wvt_ref, ckv_ref, kpe_ref, gate_ref, q_ref, k_ref, vt_ref) = refs
    else:
        (x_ref, nw_ref, win_ref, qnw_ref, kvnw_ref, wq_ref, cosq_ref, sinq_ref, cosk_ref, sink_ref,
         wukt_ref, ckv_ref, kpe_ref, gate_ref, qlat_ref, qpe_ref) = refs
    xn = _rms(x_ref[...], nw_ref[...]).astype(BF16)
    o1 = q_lora
    o2 = o1 + kv_lora
    cq = _rms(_dot(xn, win_ref[:, 0:o1]), qnw_ref[...]).astype(BF16)
    ckv = _rms(_dot(xn, win_ref[:, o1:o2]), kvnw_ref[...])
    ckv_ref[...] = ckv
    kp = _dot(xn, win_ref[:, o2:o2 + LANES])
    kpe = kp * cosk_ref[...] + _rotate_half(kp, 0, rope // 2) * sink_ref[...]
    kpe_ref[...] = kpe[:, :rope]
    gate_ref[...] = _dot(xn, win_ref[:, o2 + LANES:])
    hc = 8
    cosq = jnp.concatenate([cosq_ref[...] * scale] * hc, axis=1)
    sinq = jnp.concatenate([sinq_ref[...] * scale] * hc, axis=1)
    for h0 in range(0, heads, hc):
        cs = slice(h0 * LANES, (h0 + hc) * LANES)
        qa = _dot(cq, wq_ref[:, cs])
        qc = qa * cosq + _rotate_half(qa, nope, rope // 2) * sinq
        if prompt:
            q_ref[:, cs] = qc.astype(BF16)
        else:
            for h in range(h0, h0 + hc):
                off = (h - h0) * LANES
                qlat_ref[h] = _dot(qc[:, off:off + nope].astype(BF16), wukt_ref[h])
                qpe_ref[h] = qc[:, off + nope:off + nope + rope]
    if prompt:
        ckv16 = ckv.astype(BF16)
        kin = jnp.concatenate([ckv16, kpe.astype(BF16)], axis=1)
        for c0 in range(0, heads * LANES, 1024):
            k_ref[:, c0:c0 + 1024] = _dot(kin, wk_ref[:, c0:c0 + 1024]).astype(BF16)
        vt = _dot_nt(wvt_ref[...], ckv16)
        vhead = vt.shape[0] // heads
        npad = vt_ref.shape[1] // heads - vhead
        pad = (lax.broadcasted_iota(jnp.int32, (npad, vt.shape[1]), 0) == 0).astype(F32)
        pieces = []
        for h in range(heads):
            pieces += [vt[h * vhead:(h + 1) * vhead], pad]
        vt_ref[0] = jnp.concatenate(pieces, axis=0).astype(BF16)


def _mla_proj(h_all, row0_blocks, ntok, tm, seq_blocks, nw, w, tables, dims, prompt):
    heads, q_lora, kv_lora, rope, nope, vhead, scale = dims
    d = h_all.shape[1]
    cosq, sinq, cosk, sink = tables
    tok = lambda i: (row0_blocks + i, 0)
    out_tok = lambda i: (i, 0)
    pos = lambda i: (i % seq_blocks, 0)
    shared = [w["in"], w["qn"], w["kvn"], w["q"]]
    in_specs = [pl.BlockSpec((tm, d), tok), _const_spec(nw.shape)] + [_const_spec(x.shape) for x in shared]
    in_specs += [pl.BlockSpec((tm, LANES), pos)] * 4
    args = [h_all, nw] + shared + [cosq, sinq, cosk, sink]
    gw = heads * vhead
    out_specs = [pl.BlockSpec((tm, kv_lora), out_tok), pl.BlockSpec((tm, rope), out_tok), pl.BlockSpec((tm, gw), out_tok)]
    out_shape = [jax.ShapeDtypeStruct((ntok, kv_lora), F32), jax.ShapeDtypeStruct((ntok, rope), F32),
                 jax.ShapeDtypeStruct((ntok, gw), F32)]
    if prompt:
        nb = ntok // (seq_blocks * tm)
        in_specs += [_const_spec(w["k"].shape), _const_spec(w["vt"].shape)]
        args += [w["k"], w["vt"]]
        vr = heads * _vt_rows(vhead)
        out_specs += [pl.BlockSpec((tm, heads * LANES), out_tok), pl.BlockSpec((tm, heads * LANES), out_tok),
                      pl.BlockSpec((1, vr, tm), lambda i: (i // seq_blocks, 0, i % seq_blocks))]
        out_shape += [jax.ShapeDtypeStruct((ntok, heads * LANES), BF16), jax.ShapeDtypeStruct((ntok, heads * LANES), BF16),
                      jax.ShapeDtypeStruct((nb, vr, seq_blocks * tm), BF16)]
    else:
        in_specs += [_const_spec(w["ukt"].shape)]
        args += [w["ukt"]]
        out_specs += [pl.BlockSpec((heads, tm, kv_lora), lambda i: (0, i, 0)),
                      pl.BlockSpec((heads, tm, rope), lambda i: (0, i, 0))]
        out_shape += [jax.ShapeDtypeStruct((heads, ntok, kv_lora), F32), jax.ShapeDtypeStruct((heads, ntok, rope), F32)]
    return pl.pallas_call(
        functools.partial(_mla_proj_kernel, prompt=prompt, heads=heads, q_lora=q_lora, kv_lora=kv_lora, rope=rope,
                          nope=nope, scale=scale),
        grid=(ntok // tm,),
        in_specs=in_specs,
        out_specs=out_specs,
        out_shape=out_shape,
        compiler_params=_cparams("parallel"),
        name="mla_proj_prompt" if prompt else "mla_proj_sample",
    )(*args)


def _vt_rows(vhead):
    bf16_sublanes = 16
    return -(-(vhead + 1) // bf16_sublanes) * bf16_sublanes


def _prompt_attn_kernel(qi_ref, ki_ref, q_ref, k_ref, vt_ref, gate_ref, o_ref, m_ref, acc_ref, *,
                        heads, vhead, tile):
    p = pl.program_id(1)
    qi = qi_ref[p]
    ki = ki_ref[p]
    vrows = _vt_rows(vhead)

    @pl.when(ki == 0)
    def _init():
        m_ref[...] = jnp.full(m_ref.shape, NEG_INF, F32)
        acc_ref[...] = jnp.zeros(acc_ref.shape, F32)

    def step(diagonal):
        if diagonal:
            visible = (lax.broadcasted_iota(jnp.int32, (tile, tile), 0)
                       <= lax.broadcasted_iota(jnp.int32, (tile, tile), 1))

        def scores(h):
            sl = slice(h * LANES, (h + 1) * LANES)
            st = _dot_nt(k_ref[:, sl], q_ref[:, sl])
            return jnp.where(visible, st, NEG_INF) if diagonal else st

        def new_max(h, st):
            m_prev = m_ref[h]
            return m_prev, jnp.maximum(m_prev, jnp.max(st, axis=0, keepdims=True))

        def accumulate(h, st, m_prev, m_cur):
            alpha = jnp.exp2(m_prev - m_cur)
            pt = jnp.exp2(st - m_cur).astype(BF16)
            acc_ref[h] = acc_ref[h] * alpha + _dot(vt_ref[0, h * vrows:(h + 1) * vrows, :], pt)
            m_ref[h] = m_cur

        sts = {h: scores(h) for h in range(min(SCORE_AHEAD + 1, heads))}
        maxes = {0: new_max(0, sts[0])}
        for h in range(heads):
            if h + SCORE_AHEAD + 1 < heads:
                sts[h + SCORE_AHEAD + 1] = scores(h + SCORE_AHEAD + 1)
            if h + 1 < heads:
                maxes[h + 1] = new_max(h + 1, sts[h + 1])
            accumulate(h, sts.pop(h), *maxes.pop(h))

    pl.when(ki < qi)(lambda: step(False))

    @pl.when(ki == qi)
    def _last():
        step(True)
        ot = jnp.concatenate([acc_ref[h, :vhead, :] / acc_ref[h, vhead:vhead + 1, :] for h in range(heads)], axis=0)
        o_ref[...] = ot.T * _silu(gate_ref[...])


def _prompt_attn(q, k, vt, gate, nb, seq, heads, vhead, tile):
    nq = seq // tile
    pairs = [(a, b) for a in range(nq) for b in range(a + 1)]
    qi_tab = jnp.asarray([a for a, _ in pairs], jnp.int32)
    ki_tab = jnp.asarray([b for _, b in pairs], jnp.int32)
    vrows = _vt_rows(vhead)
    grid_spec = pltpu.PrefetchScalarGridSpec(
        num_scalar_prefetch=2,
        grid=(nb, len(pairs)),
        in_specs=[pl.BlockSpec((tile, heads * LANES), lambda b, p, qt, kt: (b * nq + qt[p], 0)),
                  pl.BlockSpec((tile, heads * LANES), lambda b, p, qt, kt: (b * nq + kt[p], 0)),
                  pl.BlockSpec((1, heads * vrows, tile), lambda b, p, qt, kt: (b, 0, kt[p])),
                  pl.BlockSpec((tile, heads * vhead), lambda b, p, qt, kt: (b * nq + qt[p], 0))],
        out_specs=pl.BlockSpec((tile, heads * vhead), lambda b, p, qt, kt: (b * nq + qt[p], 0)),
        scratch_shapes=[pltpu.VMEM((heads, 1, tile), F32),
                        pltpu.VMEM((heads, vrows, tile), F32)])
    return pl.pallas_call(
        functools.partial(_prompt_attn_kernel, heads=heads, vhead=vhead, tile=tile),
        grid_spec=grid_spec,
        out_shape=jax.ShapeDtypeStruct((nb * seq, heads * vhead), F32),
        compiler_params=_cparams("parallel", "arbitrary"),
        name="mla_prompt_attn",
    )(qi_tab, ki_tab, q, k, vt, gate)


def _sample_attn_kernel(pt_ref, qlat_ref, qpe_ref, ckvn_ref, kpen_ref, ckv_hbm, kpe_hbm, o_ref,
                        kbuf, pbuf, sem, m_ref, l_ref, acc_ref, *, layer, npages, heads, tnew, page, nchains):
    b = pl.program_id(0)
    s = pl.program_id(1)
    nb = pl.num_programs(0)
    ns = pl.num_programs(1)
    t = b * ns + s
    slot = lax.rem(t, 2)
    rows = heads * tnew

    def page_copies(bb, ss, sl):
        out = []
        for i in range(npages):
            pid = pt_ref[bb, ss * npages + i]
            out.append(pltpu.make_async_copy(ckv_hbm.at[layer, pid], kbuf.at[sl, pl.ds(i * page, page)], sem.at[0, sl]))
            out.append(pltpu.make_async_copy(kpe_hbm.at[layer, pid], pbuf.at[sl, :, pl.ds(i * page, page)], sem.at[1, sl]))
        return out

    @pl.when(t == 0)
    def _first():
        for c in page_copies(b, s, slot):
            c.start()

    @pl.when(t + 1 < nb * ns)
    def _prefetch():
        wrap = s + 1 == ns
        for c in page_copies(jnp.where(wrap, b + 1, b), jnp.where(wrap, 0, s + 1), 1 - slot):
            c.start()

    q = qlat_ref[...].reshape(rows, qlat_ref.shape[2]).astype(BF16)
    qp = qpe_ref[...].reshape(rows, qpe_ref.shape[2]).astype(BF16)

    @pl.when(s == 0)
    def _new_tokens():
        cn = ckvn_ref[...].astype(BF16)
        sc = _dot_nt(q, cn) + _dot_nt(qp, kpen_ref[...].astype(BF16))
        tq_ = lax.broadcasted_iota(jnp.int32, (rows, tnew), 0) % tnew
        tk_ = lax.broadcasted_iota(jnp.int32, (rows, tnew), 1)
        sc = jnp.where(tk_ <= tq_, sc, NEG_INF)
        m = jnp.max(sc, axis=-1, keepdims=True)
        pr = jnp.exp2(sc - m)
        m_ref[0] = m
        l_ref[0] = jnp.sum(pr, axis=-1, keepdims=True)
        acc_ref[0] = _dot(pr, ckvn_ref[...])
        for c in range(1, nchains):
            m_ref[c] = jnp.full((rows, 1), NEG_INF, F32)
            l_ref[c] = jnp.zeros((rows, 1), F32)
            acc_ref[c] = jnp.zeros(acc_ref.shape[1:], F32)

    for c in page_copies(b, s, slot):
        c.wait()

    nkeys = npages * page // nchains
    cks, scs = [], []
    for c in range(nchains):
        ck = kbuf[slot, pl.ds(c * nkeys, nkeys), :].astype(BF16)
        kp = pbuf[slot, :, pl.ds(c * nkeys, nkeys)].astype(BF16)
        cks.append(ck)
        scs.append(_dot_nt(q, ck) + _dot(qp, kp))
    for c in range(nchains):
        m_prev = m_ref[c]
        m_cur = jnp.maximum(m_prev, jnp.max(scs[c], axis=-1, keepdims=True))
        alpha = jnp.exp2(m_prev - m_cur)
        pr = jnp.exp2(scs[c] - m_cur)
        l_ref[c] = alpha * l_ref[c] + jnp.sum(pr, axis=-1, keepdims=True)
        acc_ref[c] = acc_ref[c] * alpha + _dot(pr.astype(BF16), cks[c])
        m_ref[c] = m_cur

    @pl.when(s == ns - 1)
    def _fin():
        m = m_ref[0]
        for c in range(1, nchains):
            m = jnp.maximum(m, m_ref[c])
        l = jnp.zeros((rows, 1), F32)
        acc = jnp.zeros(acc_ref.shape[1:], F32)
        for c in range(nchains):
            w = jnp.exp2(m_ref[c] - m)
            l = l + l_ref[c] * w
            acc = acc + acc_ref[c] * w
        o_ref[0] = acc / l


def _sample_attn(page_table, qlat, qpe, ckv_new, kpe_new, cache_ckv, cache_kpe_t, layer, npages, nchains):
    heads, ntok, kv_lora = qlat.shape
    rope = qpe.shape[2]
    nb, n_pages_total = page_table.shape
    tnew = ntok // nb
    page = cache_ckv.shape[2]
    steps = n_pages_total // npages
    rows = heads * tnew
    in_specs = [pl.BlockSpec((heads, tnew, kv_lora), lambda b, s, pt: (0, b, 0)),
                pl.BlockSpec((heads, tnew, rope), lambda b, s, pt: (0, b, 0)),
                pl.BlockSpec((tnew, kv_lora), lambda b, s, pt: (b, 0)),
                pl.BlockSpec((tnew, rope), lambda b, s, pt: (b, 0)),
                pl.BlockSpec(memory_space=pl.ANY),
                pl.BlockSpec(memory_space=pl.ANY)]
    grid_spec = pltpu.PrefetchScalarGridSpec(
        num_scalar_prefetch=1,
        grid=(nb, steps),
        in_specs=in_specs,
        out_specs=pl.BlockSpec((1, rows, kv_lora), lambda b, s, pt: (b, 0, 0)),
        scratch_shapes=[pltpu.VMEM((2, npages * page, kv_lora), F32),
                        pltpu.VMEM((2, rope, npages * page), F32),
                        pltpu.SemaphoreType.DMA((2, 2)),
                        pltpu.VMEM((nchains, rows, 1), F32), pltpu.VMEM((nchains, rows, 1), F32),
                        pltpu.VMEM((nchains, rows, kv_lora), F32)])
    return pl.pallas_call(
        functools.partial(_sample_attn_kernel, layer=layer, npages=npages, heads=heads, tnew=tnew, page=page,
                          nchains=nchains),
        grid_spec=grid_spec,
        out_shape=jax.ShapeDtypeStruct((nb, rows, kv_lora), F32),
        compiler_params=_cparams("arbitrary", "arbitrary"),
        name="mla_sample_attn",
    )(page_table, qlat, qpe, ckv_new, kpe_new, cache_ckv, cache_kpe_t)


def _sample_post_kernel(ol_ref, wuv_ref, gate_ref, o_ref, *, heads, tnew, vhead):
    bt = ol_ref.shape[0]
    for h in range(heads):
        x = ol_ref[:, h * tnew:(h + 1) * tnew, :].reshape(bt * tnew, ol_ref.shape[2]).astype(BF16)
        g = gate_ref[:, h * vhead:(h + 1) * vhead]
        o_ref[:, h * vhead:(h + 1) * vhead] = _dot(x, wuv_ref[h]) * _silu(g)


def _sample_post(olat, wuv, gate, tnew, bt):
    nb, rows, kv_lora = olat.shape
    heads, _, vhead = wuv.shape
    ntok = nb * tnew
    return pl.pallas_call(
        functools.partial(_sample_post_kernel, heads=heads, tnew=tnew, vhead=vhead),
        grid=(nb // bt,),
        in_specs=[pl.BlockSpec((bt, rows, kv_lora), lambda i: (i, 0, 0)),
                  _const_spec(wuv.shape),
                  pl.BlockSpec((bt * tnew, heads * vhead), lambda i: (i, 0))],
        out_specs=pl.BlockSpec((bt * tnew, heads * vhead), lambda i: (i, 0)),
        out_shape=jax.ShapeDtypeStruct((ntok, heads * vhead), F32),
        compiler_params=_cparams("parallel"),
        name="mla_sample_post",
    )(olat, wuv, gate)


def _rope_tables(pos, rope, nope):
    inv = 1.0 / (ROPE_BASE ** (jnp.arange(0, rope, 2, dtype=F32) / rope))
    ang = pos[:, None] * inv[None, :]
    cos, sin = jnp.cos(ang), jnp.sin(ang)
    n = pos.shape[0]
    cc = jnp.concatenate([cos, cos], axis=1)
    ss = jnp.concatenate([sin, sin], axis=1)
    cosk = jnp.pad(cc, ((0, 0), (0, LANES - rope)))
    sink = jnp.pad(ss, ((0, 0), (0, LANES - rope)))
    cosq = jnp.concatenate([jnp.ones((n, nope), F32), cc, jnp.zeros((n, LANES - nope - rope), F32)], axis=1)
    sinq = jnp.concatenate([jnp.zeros((n, nope), F32), ss, jnp.zeros((n, LANES - nope - rope), F32)], axis=1)
    return cosq, sinq, cosk, sink


def _mla_weights(w_in, qn, kvn, w_uq, w_uk, w_uv, q_lora, kv_lora, rope, nope):
    heads = w_uq.shape[1]
    vhead = w_uv.shape[2]
    o2, o3 = q_lora + kv_lora, q_lora + kv_lora + rope
    w_in_big = jnp.concatenate([w_in[:, :o2], jnp.pad(w_in[:, o2:o3], ((0, 0), (0, LANES - rope))), w_in[:, o3:]], axis=1)
    zpad = jnp.zeros((q_lora, heads, LANES - nope - rope), F32)
    wq = jnp.concatenate([w_uq, zpad], axis=2).reshape(q_lora, heads * LANES)
    wk_lat = jnp.concatenate([w_uk, jnp.zeros((kv_lora, heads, LANES - nope), F32)], axis=2).reshape(kv_lora, heads * LANES)
    place = jnp.zeros((LANES, heads, LANES), F32).at[jnp.arange(rope), :, nope + jnp.arange(rope)].set(1.0)
    wk = jnp.concatenate([wk_lat, place.reshape(LANES, heads * LANES)], axis=0)
    wvt = w_uv.transpose(1, 2, 0).reshape(heads * vhead, kv_lora)
    return {"in": w_in_big.astype(BF16), "qn": qn.reshape(1, -1), "kvn": kvn.reshape(1, -1),
            "q": wq.astype(BF16), "k": wk.astype(BF16), "vt": wvt.astype(BF16),
            "ukt": w_uk.transpose(1, 2, 0).astype(BF16), "uv": w_uv.transpose(1, 0, 2).astype(BF16)}


def kernel(x_prompt, x_sample, state_ssm, state_conv, cache_ckv, cache_kpe, page_table, norm_w, final_norm_w,
           ssd_w_in, ssd_conv_w, ssd_conv_b, ssd_dt_bias, ssd_a_log, ssd_d, ssd_norm_w, ssd_w_out,
           mla_w_in, mla_q_norm_w, mla_kv_norm_w, mla_w_uq, mla_w_uk, mla_w_uv, mla_w_out):
    nb, seq, d = x_prompt.shape
    db, dseq, _ = x_sample.shape
    depth = norm_w.shape[0]
    tp, ts = nb * seq, db * dseq
    _, _, heads_s, headdim, state = state_ssm.shape
    d_inner = heads_s * headdim
    conv_dim = ssd_conv_w.shape[2]
    gn = (conv_dim - d_inner) // 2
    groups = gn // state
    q_lora, heads_m, qk = mla_w_uq.shape[1:]
    kv_lora = mla_w_uk.shape[1]
    nope = mla_w_uk.shape[3]
    rope = qk - nope
    vhead = mla_w_uv.shape[3]
    past = page_table.shape[1] * cache_ckv.shape[2]
    scale = (nope + rope) ** -0.5 * LOG2E

    assert seq % SSD_CHUNK == 0 and dseq <= SSD_CHUNK and tp % dseq == 0
    tm = 512
    assert tp % tm == 0 and ts % tm == 0

    h_all = (x_prompt.reshape(tp, d), x_sample.reshape(ts, d))

    e = (jnp.arange(d_inner)[None, :] // headdim == jnp.arange(LANES)[:, None]).astype(BF16)
    e = jnp.concatenate([e, e], axis=0)
    tab_p = _rope_tables(jnp.arange(seq, dtype=F32), rope, nope)
    tab_s = _rope_tables(jnp.tile(past + jnp.arange(dseq, dtype=F32), ts // dseq), rope, nope)
    mla_dims = (heads_m, q_lora, kv_lora, rope, nope, vhead, scale)

    cache_kpe_t = jnp.swapaxes(cache_kpe, 2, 3)
    n_ssd = (depth + 1) // 2
    tail = SSD_CONV_W - 1

    p_conv, p_ckv, p_kpe = [], [], []
    s_conv, s_ckv, s_kpe = [], [], []
    p_ssm = s_ssm = None
    for i in range(depth):
        j = i // 2
        nw = norm_w[i].reshape(1, d)
        fin_w = final_norm_w.reshape(1, d) if i == depth - 1 else None
        if i % 2 == 0:
            w_big = jnp.pad(ssd_w_in[j], ((0, 0), (0, LANES - heads_s))).astype(BF16)
            z, xbc, dt = _ssd_in_proj(h_all, nw, w_big, d_inner, conv_dim, tm)
            params = (ssd_conv_w[j], ssd_conv_b[j].reshape(1, -1),
                      jnp.pad(ssd_dt_bias[j], (0, LANES - heads_s)).reshape(1, LANES),
                      jnp.pad(ssd_a_log[j], (0, LANES - heads_s)).reshape(1, LANES),
                      jnp.repeat(ssd_d[j], headdim).reshape(1, d_inner),
                      ssd_norm_w[j].reshape(1, d_inner), e, (heads_s, headdim, groups, state))
            y_p, p_ssm = _ssd_scan(xbc, z, dt, 0, nb, seq // SSD_CHUNK, SSD_CHUNK, params, j, n_ssd,
                                   None, None, p_ssm)
            y_s, s_ssm = _ssd_scan(xbc, z, dt, tp // dseq, db, 1, dseq, params, j, n_ssd,
                                   state_conv, state_ssm, s_ssm)
            h_all = _out_proj(y_p, y_s, ssd_w_out[j].astype(BF16), h_all, fin_w, tm)
            p_conv.append(jnp.stack([xbc[(b + 1) * seq - tail:(b + 1) * seq] for b in range(nb)]))
            s_conv.append(xbc[tp:].reshape(db, dseq, conv_dim)[:, dseq - tail:])
        else:
            w = _mla_weights(mla_w_in[j], mla_q_norm_w[j], mla_kv_norm_w[j], mla_w_uq[j], mla_w_uk[j], mla_w_uv[j],
                             q_lora, kv_lora, rope, nope)
            tmp = 512
            ckv_p, kpe_p, gate_p, q_p, k_p, vt_p = _mla_proj(h_all, 0, tp, tmp, seq // tmp, nw, w, tab_p, mla_dims, True)
            ckv_s, kpe_s, gate_s, qlat_s, qpe_s = _mla_proj(h_all, tp // tmp, ts, tmp, ts // tmp, nw, w, tab_s,
                                                            mla_dims, False)
            og_p = _prompt_attn(q_p, k_p, vt_p, gate_p, nb, seq, heads_m, vhead, math.gcd(ATTN_TILE, seq))
            npg = math.gcd(PAGES_PER_STEP, page_table.shape[1])
            olat = _sample_attn(page_table, qlat_s, qpe_s, ckv_s, kpe_s, cache_ckv, cache_kpe_t, j, npg, DECODE_CHAINS)
            og_s = _sample_post(olat, w["uv"], gate_s, dseq, math.gcd(16, db))
            h_all = _out_proj(og_p, og_s, mla_w_out[j].astype(BF16), h_all, fin_w, tm)
            p_ckv.append(ckv_p.reshape(nb, seq, kv_lora))
            p_kpe.append(kpe_p.reshape(nb, seq, rope))
            s_ckv.append(ckv_s.reshape(db, dseq, kv_lora))
            s_kpe.append(kpe_s.reshape(db, dseq, rope))
    y_prompt = h_all[0].reshape(nb, seq, d)
    y_sample = h_all[1].reshape(db, dseq, d)
    return (y_prompt, y_sample,
            p_ssm, jnp.stack(p_conv), jnp.stack(p_ckv), jnp.stack(p_kpe),
            s_ssm, jnp.stack(s_conv), jnp.stack(s_ckv), jnp.stack(s_kpe))
```
